```python
import jax, jax.numpy as jnp
from jax import lax
import numpy as np

D_MODEL = 1024
BATCH = 2
SEQ = 8192
DEPTH = 4

GRID_W = 64
CTX_LEN = 256
HEAD_DIM = 64
N_Q_HEADS = 8
N_KV_HEADS = 2
GROUP = N_Q_HEADS // N_KV_HEADS
ATTN_WIDTH = N_Q_HEADS * HEAD_DIM
KV_WIDTH = N_KV_HEADS * HEAD_DIM
AXIS_DIM = HEAD_DIM // 2
ROPE_THETA = 10000.0
Q_BLOCK = 128
CONV_WIDTH = D_MODEL // 4
CONV_GROUPS = 4
CONV_K = 3
CHUNK = 128
SG_GROUPS = 4
SG_WIDTH = D_MODEL // 4
N_BRANCH = 3
D_FF = -(-8 * D_MODEL // (3 * 256)) * 256
N_MOD = 6
EPS = 1e-6

OFF_Q = 3 * CONV_WIDTH
OFF_K = OFF_Q + ATTN_WIDTH
OFF_V = OFF_K + KV_WIDTH
OFF_U = OFF_V + KV_WIDTH
OFF_SV = OFF_U + SG_WIDTH
OFF_G = OFF_SV + SG_WIDTH
IN_WIDTH = OFF_G + N_BRANCH * D_MODEL

kernel_name = 'hybrid_conv_gqa_gmlp_dit_block'


def rms_norm(x, g):
    xf = x.astype(jnp.float32)
    y = xf * lax.rsqrt(jnp.mean(xf * xf, axis=-1, keepdims=True) + EPS)
    return (y * g.astype(jnp.float32)).astype(x.dtype)


def modulate(h, shift, scale):
    return h * (1 + scale) + shift


def adaln(cond, w_mod, b_mod):
    return jnp.split(jax.nn.silu(cond) @ w_mod + b_mod, N_MOD, axis=-1)


def axial_rope_tables(n):
    rows = n // GRID_W
    row = jnp.repeat(jnp.arange(rows, dtype=jnp.float32), GRID_W)
    col = jnp.tile(jnp.arange(GRID_W, dtype=jnp.float32), rows)
    inv_freq = ROPE_THETA ** (-jnp.arange(0, AXIS_DIM, 2, dtype=jnp.float32) / AXIS_DIM)
    ang = jnp.stack([row[:, None] * inv_freq, col[:, None] * inv_freq], axis=1)
    return jnp.cos(ang), jnp.sin(ang)


def apply_rope(x, cos, sin):
    xr = x.astype(jnp.float32).reshape(*x.shape[:-1], 2, 2, AXIS_DIM // 2)
    x1, x2 = xr[..., 0, :], xr[..., 1, :]
    cs, sn = cos[:, None], sin[:, None]
    out = jnp.stack([x1 * cs - x2 * sn, x2 * cs + x1 * sn], axis=-2)
    return out.reshape(x.shape).astype(x.dtype)


def short_conv(z, w):
    zp = jnp.pad(z, ((0, 0), (1, 1), (0, 0)))
    return zp[:, :-2] * w[0] + zp[:, 1:-1] * w[1] + zp[:, 2:] * w[2]


def spatial_gate(u, v, sg_norm, w_s, b_s):
    v = rms_norm(v, sg_norm)
    B, S, _ = v.shape
    vc = v.reshape(B, S // CHUNK, CHUNK, SG_GROUPS, SG_WIDTH // SG_GROUPS)
    mixed = jnp.einsum('gts,bcsgd->bctgd', w_s, vc) + b_s.T[None, None, :, :, None]
    return u * mixed.reshape(B, S, SG_WIDTH)


def gqa_attend(q, k, v):
    s = jnp.einsum('bqhgd,bkhd->bhgqk', q, k).astype(jnp.float32) * (HEAD_DIM ** -0.5)
    p = jax.nn.softmax(s, axis=-1).astype(v.dtype)
    return jnp.einsum('bhgqk,bkhd->bqhgd', p, v)


def attend_blocks(q, k, v):
    B, S = q.shape[:2]
    nb = S // Q_BLOCK
    qb = jnp.moveaxis(q.reshape(B, nb, Q_BLOCK, N_KV_HEADS, GROUP, HEAD_DIM), 1, 0)
    out = lax.map(lambda qi: gqa_attend(qi, k, v), qb)
    return jnp.moveaxis(out, 0, 1).reshape(B, S, ATTN_WIDTH)


def project(h, w_in, q_gain, k_gain):
    p = h @ w_in
    a_b, a_c, a_x, q, k, v, u, sv, g = jnp.split(
        p, (CONV_WIDTH, 2 * CONV_WIDTH, OFF_Q, OFF_K, OFF_V, OFF_U, OFF_SV, OFF_G), axis=-1)
    q = rms_norm(q.reshape(*q.shape[:-1], N_Q_HEADS, HEAD_DIM), q_gain)
    k = rms_norm(k.reshape(*k.shape[:-1], N_KV_HEADS, HEAD_DIM), k_gain)
    v = v.reshape(*v.shape[:-1], N_KV_HEADS, HEAD_DIM)
    return a_b, a_c, a_x, q, k, v, u, sv, g


def project_kv(h, w_in, k_gain):
    k, v = jnp.split(h @ w_in[:, OFF_K:OFF_U], 2, axis=-1)
    k = rms_norm(k.reshape(*k.shape[:-1], N_KV_HEADS, HEAD_DIM), k_gain)
    return k, v.reshape(*v.shape[:-1], N_KV_HEADS, HEAD_DIM)


def merge_branches(a_b, a_c, a_x, attn, u, sv, g, conv_w, sg_norm, w_s, b_s, w_a, w_b, w_c, w_o):
    y_a = (a_b * short_conv(a_c * a_x, conv_w)) @ w_a
    y_b = attn @ w_b
    y_c = spatial_gate(jax.nn.gelu(u), jax.nn.gelu(sv), sg_norm, w_s, b_s) @ w_c
    g_a, g_b, g_c = jnp.split(jax.nn.sigmoid(g), N_BRANCH, axis=-1)
    return (g_a * y_a + g_b * y_b + g_c * y_c) @ w_o


def swiglu(h, w1, w3, w2):
    return (jax.nn.silu(h @ w1) * (h @ w3)) @ w2


def setup_inputs(seed: int = 0) -> dict:
    key = jax.random.key(seed)
    ks = jax.random.split(key, 24)
    f = jnp.float32
    D = D_MODEL

    def nrm(k, shape, scale):
        return jax.random.normal(k, shape, f) * scale

    return {
        'x': nrm(ks[0], (BATCH, SEQ, D), 1.0),
        'c': nrm(ks[1], (BATCH, D), 1.0),
        'ctx': nrm(ks[2], (BATCH, CTX_LEN, D), 1.0),
        'c_ctx': nrm(ks[3], (D,), 1.0),
        'w_mod': nrm(ks[4], (DEPTH, D, N_MOD * D), 0.5 * D ** -0.5),
        'b_mod': nrm(ks[5], (DEPTH, N_MOD * D), 0.02),
        'norm1': 1.0 + nrm(ks[6], (DEPTH, D), 0.02),
        'w_in': nrm(ks[7], (DEPTH, D, IN_WIDTH), D ** -0.5),
        'q_gain': 1.0 + nrm(ks[8], (DEPTH, HEAD_DIM), 0.02),
        'k_gain': 1.0 + nrm(ks[9], (DEPTH, HEAD_DIM), 0.02),
        'conv_w': nrm(ks[10], (DEPTH, CONV_K, CONV_WIDTH), CONV_K ** -0.5),
        'sg_norm': 1.0 + nrm(ks[11], (DEPTH, SG_WIDTH), 0.02),
        'w_s': nrm(ks[12], (DEPTH, SG_GROUPS, CHUNK, CHUNK), CHUNK ** -0.5),
        'b_s': 1.0 + nrm(ks[13], (DEPTH, SG_GROUPS, CHUNK), 0.02),
        'w_a': nrm(ks[14], (DEPTH, CONV_WIDTH, D), CONV_WIDTH ** -0.5),
        'w_b': nrm(ks[15], (DEPTH, ATTN_WIDTH, D), ATTN_WIDTH ** -0.5),
        'w_c': nrm(ks[16], (DEPTH, SG_WIDTH, D), SG_WIDTH ** -0.5),
        'w_o': nrm(ks[17], (DEPTH, D, D), D ** -0.5),
        'norm2': 1.0 + nrm(ks[18], (DEPTH, D), 0.02),
        'w_ff1': nrm(ks[19], (DEPTH, D, D_FF), D ** -0.5),
        'w_ff3': nrm(ks[20], (DEPTH, D, D_FF), D ** -0.5),
        'w_ff2': nrm(ks[21], (DEPTH, D_FF, D), D_FF ** -0.5),
    }


def reference(x, c, ctx, c_ctx, w_mod, b_mod, norm1, w_in, q_gain, k_gain, conv_w, sg_norm,
              w_s, b_s, w_a, w_b, w_c, w_o, norm2, w_ff1, w_ff3, w_ff2):
    n = x.shape[1]
    cos, sin = axial_rope_tables(n)
    B, L = ctx.shape[:2]
    for l in range(DEPTH):
        last = l == DEPTH - 1
        sh1, sc1, gt1, sh2, sc2, gt2 = [m[:, None, :] for m in adaln(c, w_mod[l], b_mod[l])]
        csh1, csc1, cgt1, csh2, csc2, cgt2 = adaln(c_ctx, w_mod[l], b_mod[l])

        h_ctx = modulate(rms_norm(ctx, norm1[l]), csh1, csc1)
        if last:
            k_c, v_c = project_kv(h_ctx, w_in[l], k_gain[l])
        else:
            ca_b, ca_c, ca_x, q_c, k_c, v_c, cu, csv, cg = project(h_ctx, w_in[l], q_gain[l], k_gain[l])
            attn_c = gqa_attend(q_c.reshape(B, L, N_KV_HEADS, GROUP, HEAD_DIM), k_c, v_c)
            attn_c = attn_c.reshape(B, L, ATTN_WIDTH)

        h = modulate(rms_norm(x, norm1[l]), sh1, sc1)
        a_b, a_c, a_x, q, k, v, u, sv, g = project(h, w_in[l], q_gain[l], k_gain[l])
        q = apply_rope(q, cos, sin)
        k = apply_rope(k, cos, sin)
        k_all = jnp.concatenate([k, k_c], axis=1)
        v_all = jnp.concatenate([v, v_c], axis=1)
        attn = attend_blocks(q, k_all, v_all)
        x = x + gt1 * merge_branches(a_b, a_c, a_x, attn, u, sv, g, conv_w[l], sg_norm[l],
                                     w_s[l], b_s[l], w_a[l], w_b[l], w_c[l], w_o[l])
        x = x + gt2 * swiglu(modulate(rms_norm(x, norm2[l]), sh2, sc2), w_ff1[l], w_ff3[l], w_ff2[l])

        if not last:
            ctx = ctx + cgt1 * merge_branches(ca_b, ca_c, ca_x, attn_c, cu, csv, cg, conv_w[l],
                                              sg_norm[l], w_s[l], b_s[l], w_a[l], w_b[l],
                                              w_c[l], w_o[l])
            ctx = ctx + cgt2 * swiglu(modulate(rms_norm(ctx, norm2[l]), csh2, csc2),
                                      w_ff1[l], w_ff3[l], w_ff2[l])
    return x
```

```python
import functools
import math

import jax
import jax.numpy as jnp
from jax import lax
from jax.experimental import pallas as pl
from jax.experimental.pallas import tpu as pltpu

F32 = jnp.float32
BF16 = jnp.bfloat16

GRID_W = 64
HEAD_DIM = 64
N_Q_HEADS = 8
N_KV_HEADS = 2
GROUP = N_Q_HEADS // N_KV_HEADS
ATTN_WIDTH = N_Q_HEADS * HEAD_DIM
KV_WIDTH = N_KV_HEADS * HEAD_DIM
AXIS_DIM = HEAD_DIM // 2
ROPE_THETA = 10000.0
CONV_K = 3
CHUNK = 128
SG_GROUPS = 4
N_BRANCH = 3
N_MOD = 6
EPS = 1e-6

LANES = 128
SUBLANES = 8
TM = 256
TK = 512
MOD_ROWS = 8
VMEM_LIMIT = 56 * 1024 * 1024


def _cparams(n_axes=1):
    return pltpu.CompilerParams(dimension_semantics=("arbitrary",) * n_axes,
                                vmem_limit_bytes=VMEM_LIMIT)


def _dot(a, b):
    return jnp.dot(a, b, preferred_element_type=F32)


def _dot_t(a, b):
    return lax.dot_general(a, b, (((1,), (1,)), ((), ())), preferred_element_type=F32)


def _mod_kernel(cond_ref, w_ref, b_ref, o_ref):
    cnd = cond_ref[...]
    s = (cnd * jax.nn.sigmoid(cnd)).astype(BF16)
    o_ref[...] = _dot(s, w_ref[...].astype(BF16)) + b_ref[...]


def _adaln_all(cond, w_mod, b_mod):
    depth, d, width = w_mod.shape
    tn = width // 4
    return pl.pallas_call(
        _mod_kernel,
        grid=(depth, width // tn),
        in_specs=[
            pl.BlockSpec((MOD_ROWS, d), lambda l, j: (0, 0)),
            pl.BlockSpec((None, d, tn), lambda l, j: (l, 0, j)),
            pl.BlockSpec((None, 1, tn), lambda l, j: (l, 0, j)),
        ],
        out_specs=pl.BlockSpec((None, MOD_ROWS, tn), lambda l, j: (l, 0, j)),
        out_shape=jax.ShapeDtypeStruct((depth, MOD_ROWS, width), F32),
        compiler_params=_cparams(2),
        name="adaln",
    )(cond, w_mod, b_mod.reshape(depth, 1, width))


def _rms(x, eps=EPS):
    return x * lax.rsqrt(jnp.mean(x * x, axis=-1, keepdims=True) + eps)


def _gelu_tanh(x):
    c = math.sqrt(2.0 / math.pi)
    return 0.5 * x * (1.0 + jnp.tanh(c * (x + 0.044715 * (x * x * x))))


def _head_mean_sq(p, e):
    sq = p * p
    hi = sq.astype(BF16)
    lo = (sq - hi.astype(F32)).astype(BF16)
    return _dot(hi, e) + _dot(lo, e)


def _rope_slab(t, cos, sa, sb):
    up = pltpu.roll(t, LANES - 16, axis=1)
    dn = pltpu.roll(t, 16, axis=1)
    return t * cos + up * sa + dn * sb


def _proj_kernel(x_ref, mod_ref, n1_ref, w_ref, qg_ref, kg_ref, sgn_ref, rope_ref, e_ref,
                 ab_ref, z_ref, q_ref, kk_ref, vv_ref, gu_ref, svn_ref, gate_ref, *, d, cw, sgw):
    off_q = 3 * cw
    off_k = off_q + ATTN_WIDTH
    off_v = off_k + KV_WIDTH
    off_u = off_v + KV_WIDTH
    off_g = off_u + 2 * sgw

    x = x_ref[...]
    shift = mod_ref[:, 0:d]
    scale = mod_ref[:, d:2 * d]
    h = (_rms(x) * n1_ref[...]) * (1.0 + scale) + shift
    hb = h.astype(BF16)

    pc = _dot(hb, w_ref[:, 0:off_q])
    ab_ref[...] = pc[:, 0:cw]
    z_ref[...] = pc[:, cw:2 * cw] * pc[:, 2 * cw:3 * cw]

    cos = rope_ref[0]
    sa = rope_ref[1]
    sb = rope_ref[2]
    e = e_ref[...]

    pq = _dot(hb, w_ref[:, off_q:off_k])
    qn = pq * lax.rsqrt(_head_mean_sq(pq, e) + EPS) * qg_ref[...]
    for s in range(ATTN_WIDTH // LANES):
        r = _rope_slab(qn[:, s * LANES:(s + 1) * LANES], cos, sa, sb)
        q_ref[:, s * LANES:(s + 1) * LANES] = (r * (HEAD_DIM ** -0.5)).astype(BF16)

    pk = _dot(hb, w_ref[:, off_k:off_v])
    kn = pk * lax.rsqrt(_head_mean_sq(pk, e[0:KV_WIDTH, 0:KV_WIDTH]) + EPS) * kg_ref[...]
    kr = _rope_slab(kn, cos, sa, sb)
    ksw = pltpu.roll(kr, HEAD_DIM, axis=1)
    lane = lax.broadcasted_iota(jnp.int32, kr.shape, 1)
    lo_half = lane < HEAD_DIM
    zero = jnp.zeros_like(kr)
    kk_ref[:, 0 * LANES:1 * LANES] = jnp.where(lo_half, kr, zero).astype(BF16)
    kk_ref[:, 1 * LANES:2 * LANES] = jnp.where(lo_half, zero, ksw).astype(BF16)
    kk_ref[:, 2 * LANES:3 * LANES] = jnp.where(lo_half, ksw, zero).astype(BF16)
    kk_ref[:, 3 * LANES:4 * LANES] = jnp.where(lo_half, zero, kr).astype(BF16)

    pv = _dot(hb, w_ref[:, off_v:off_u])
    vsw = pltpu.roll(pv, HEAD_DIM, axis=1)
    one = jnp.ones_like(pv)
    vv_ref[:, 0:LANES] = jnp.where(lo_half, pv, one).astype(BF16)
    vv_ref[:, LANES:2 * LANES] = jnp.where(lo_half, vsw, one).astype(BF16)

    pu = _dot(hb, w_ref[:, off_u:off_g])
    gu_ref[...] = _gelu_tanh(pu[:, 0:sgw])
    svn_ref[...] = (_rms(_gelu_tanh(pu[:, sgw:2 * sgw])) * sgn_ref[...]).astype(BF16)

    for j in range(N_BRANCH):
        pg = _dot(hb, w_ref[:, off_g + j * d:off_g + (j + 1) * d])
        gate_ref[:, j * d:(j + 1) * d] = jax.nn.sigmoid(pg)


def _proj(xc, mods, l, norm1, w_in, qg, kg, sgn, rope, e, geo):
    ntok, d = xc.shape
    in_w = w_in.shape[-1]
    cw = d // 4
    sgw = d // 4
    n_lat, n_s, bsz = geo["n_lat"], geo["n_s"], geo["B"]
    n_t = ntok // TM

    def mod_row(i):
        return jnp.where(i < n_lat, i // n_s, bsz)

    def rope_blk(i):
        return jnp.where(i < n_lat, i % n_s, n_s)

    row = lambda w: pl.BlockSpec((TM, w), lambda i: (i, 0))
    outs = [(cw, F32), (cw, F32), (ATTN_WIDTH, BF16), (4 * LANES, BF16), (2 * LANES, BF16),
            (sgw, F32), (sgw, BF16), (N_BRANCH * d, F32)]
    return pl.pallas_call(
        functools.partial(_proj_kernel, d=d, cw=cw, sgw=sgw),
        grid=(n_t,),
        in_specs=[
            row(d),
            pl.BlockSpec((None, None, 1, N_MOD * d), lambda i: (l, mod_row(i), 0, 0)),
            pl.BlockSpec((None, 1, d), lambda i: (l, 0, 0)),
            pl.BlockSpec((None, d, in_w), lambda i: (l, 0, 0)),
            pl.BlockSpec((None, 1, ATTN_WIDTH), lambda i: (l, 0, 0)),
            pl.BlockSpec((None, 1, KV_WIDTH), lambda i: (l, 0, 0)),
            pl.BlockSpec((None, 1, sgw), lambda i: (l, 0, 0)),
            pl.BlockSpec((3, TM, LANES), lambda i: (0, rope_blk(i), 0)),
            pl.BlockSpec((ATTN_WIDTH, ATTN_WIDTH), lambda i: (0, 0)),
        ],
        out_specs=[row(w) for w, _ in outs],
        out_shape=[jax.ShapeDtypeStruct((ntok, w), dt) for w, dt in outs],
        compiler_params=_cparams(1),
        name="proj",
    )(xc, mods, norm1, w_in, qg, kg, sgn, rope, e)


def _attn_kernel(q_ref, kl_ref, vl_ref, kc_ref, vc_ref, o_ref, m_scr, acc_scr, *, n_lat, s_len):
    i = pl.program_id(0)
    tq = q_ref.shape[0]
    m_scr[...] = jnp.full(m_scr.shape, -jnp.inf, F32)
    acc_scr[...] = jnp.zeros(acc_scr.shape, F32)

    def step(k_tile, v_tile):
        for hd in range(N_Q_HEADS):
            slab, par, kvh = hd // 2, hd % 2, hd // GROUP
            qs = q_ref[:, slab * LANES:(slab + 1) * LANES]
            kcol = (2 * kvh + par) * LANES
            s = _dot_t(qs, k_tile[:, kcol:kcol + LANES])
            m_prev = m_scr[hd]
            m_new = jnp.maximum(m_prev, jnp.max(s, axis=-1, keepdims=True))
            alpha = jnp.exp(m_prev - m_new)
            p = jnp.exp(s - m_new[:, 0:1]).astype(BF16)
            pv = _dot(p, v_tile[:, kvh * LANES:(kvh + 1) * LANES])
            acc_scr[hd] = acc_scr[hd] * alpha + pv
            m_scr[hd] = m_new

    @pl.when(i < n_lat)
    def _():
        def body(j, carry):
            r0 = pl.multiple_of(j * TK, TK)
            step(kl_ref[pl.ds(r0, TK), :], vl_ref[pl.ds(r0, TK), :])
            return carry
        lax.fori_loop(0, s_len // TK, body, 0)

    step(kc_ref[...], vc_ref[...])

    lane = lax.broadcasted_iota(jnp.int32, (tq, LANES), 1)
    lo_half = lane < HEAD_DIM
    for slab in range(N_Q_HEADS // 2):
        ae = acc_scr[2 * slab]
        ao = acc_scr[2 * slab + 1]
        ne = ae / pltpu.roll(ae, HEAD_DIM, axis=1)
        no = ao / pltpu.roll(ao, HEAD_DIM, axis=1)
        out = jnp.where(lo_half, ne, pltpu.roll(no, HEAD_DIM, axis=1))
        o_ref[:, slab * LANES:(slab + 1) * LANES] = out.astype(o_ref.dtype)


def _attn(q, kk, vv, geo, n_tiles):
    n_lat, n_s, n_l, bsz = geo["n_lat"], geo["n_s"], geo["n_l"], geo["B"]
    s_len, l_len = n_s * TM, n_l * TM
    ctx_blk0 = (bsz * s_len) // l_len

    def lat_b(i):
        return jnp.minimum(i // n_s, bsz - 1)

    def ctx_b(i):
        return ctx_blk0 + jnp.where(i < n_lat, i // n_s, (i - n_lat) // n_l)

    return pl.pallas_call(
        functools.partial(_attn_kernel, n_lat=n_lat, s_len=s_len),
        grid=(n_tiles,),
        in_specs=[
            pl.BlockSpec((TM, ATTN_WIDTH), lambda i: (i, 0)),
            pl.BlockSpec((s_len, 4 * LANES), lambda i: (lat_b(i), 0)),
            pl.BlockSpec((s_len, 2 * LANES), lambda i: (lat_b(i), 0)),
            pl.BlockSpec((l_len, 4 * LANES), lambda i: (ctx_b(i), 0)),
            pl.BlockSpec((l_len, 2 * LANES), lambda i: (ctx_b(i), 0)),
        ],
        out_specs=pl.BlockSpec((TM, ATTN_WIDTH), lambda i: (i, 0)),
        out_shape=jax.ShapeDtypeStruct((n_tiles * TM, ATTN_WIDTH), BF16),
        scratch_shapes=[pltpu.VMEM((N_Q_HEADS, TM, LANES), F32),
                        pltpu.VMEM((N_Q_HEADS, TM, LANES), F32)],
        compiler_params=_cparams(1),
        name="attn",
    )(q, kk, vv, kk, vv)


def _merge_kernel(x_ref, mod_ref, ab_ref, z_ref, zp_ref, zn_ref, at_ref, gu_ref, svn_ref, gate_ref,
                  cw_ref, ws_ref, bt_ref, wa_ref, wb_ref, wc_ref, wo_ref, o_ref,
                  *, d, n_lat, n_s, n_l):
    i = pl.program_id(0)
    tm = x_ref.shape[0]
    t_in_seq = jnp.where(i < n_lat, i % n_s, (i - n_lat) % n_l)
    seq_tiles = jnp.where(i < n_lat, n_s, n_l)
    has_prev = (t_in_seq > 0).astype(F32)
    has_next = (t_in_seq < seq_tiles - 1).astype(F32)

    z = z_ref[...]
    row = lax.broadcasted_iota(jnp.int32, z.shape, 0)
    prev_row = zp_ref[SUBLANES - 1:SUBLANES, :] * has_prev
    next_row = zn_ref[0:1, :] * has_next
    zp = jnp.where(row == 0, prev_row, pltpu.roll(z, 1, axis=0))
    zn = jnp.where(row == tm - 1, next_row, pltpu.roll(z, tm - 1, axis=0))
    conv = zp * cw_ref[0:1, :] + z * cw_ref[1:2, :] + zn * cw_ref[2:3, :]
    ya = _dot((ab_ref[...] * conv).astype(BF16), wa_ref[...])

    yb = _dot(at_ref[...], wb_ref[...])

    sgw = svn_ref.shape[1]
    gw = sgw // SG_GROUPS
    lane = lax.broadcasted_iota(jnp.int32, (CHUNK, sgw), 1)
    parts = []
    for c in range(tm // CHUNK):
        vc = svn_ref[c * CHUNK:(c + 1) * CHUNK, :]
        mixed = bt_ref[...]
        for g in range(SG_GROUPS):
            in_g = (lane >= g * gw) & (lane < (g + 1) * gw)
            mixed = mixed + _dot(ws_ref[g], jnp.where(in_g, vc, jnp.zeros_like(vc)))
        parts.append(gu_ref[c * CHUNK:(c + 1) * CHUNK, :] * mixed)
    yc = _dot(jnp.concatenate(parts, axis=0).astype(BF16), wc_ref[...])

    y = gate_ref[:, 0:d] * ya + gate_ref[:, d:2 * d] * yb + gate_ref[:, 2 * d:3 * d] * yc
    out = _dot(y.astype(BF16), wo_ref[...])
    o_ref[...] = x_ref[...] + mod_ref[:, 2 * d:3 * d] * out


def _merge(xc, mods, l, ab, z, attn, gu, svn, gate, conv_w, w_s, bias_tab, w_a, w_b, w_c, w_o,
           geo, n_tiles):
    d = xc.shape[1]
    cw = ab.shape[1]
    sgw = gu.shape[1]
    n_lat, n_s, n_l, bsz = geo["n_lat"], geo["n_s"], geo["n_l"], geo["B"]
    rows_per_tile = TM // SUBLANES
    last_blk = z.shape[0] // SUBLANES - 1

    def mod_row(i):
        return jnp.where(i < n_lat, i // n_s, bsz)

    row = lambda w: pl.BlockSpec((TM, w), lambda i: (i, 0))
    whole = lambda a, b: pl.BlockSpec((None, a, b), lambda i: (l, 0, 0))
    return pl.pallas_call(
        functools.partial(_merge_kernel, d=d, n_lat=n_lat, n_s=n_s, n_l=n_l),
        grid=(n_tiles,),
        in_specs=[
            row(d),
            pl.BlockSpec((None, None, 1, N_MOD * d), lambda i: (l, mod_row(i), 0, 0)),
            row(cw),
            row(cw),
            pl.BlockSpec((SUBLANES, cw), lambda i: (jnp.maximum(i * rows_per_tile - 1, 0), 0)),
            pl.BlockSpec((SUBLANES, cw),
                         lambda i: (jnp.minimum((i + 1) * rows_per_tile, last_blk), 0)),
            row(ATTN_WIDTH),
            row(sgw),
            row(sgw),
            row(N_BRANCH * d),
            whole(CONV_K, cw),
            pl.BlockSpec((None, SG_GROUPS, CHUNK, CHUNK), lambda i: (l, 0, 0, 0)),
            whole(CHUNK, sgw),
            whole(cw, d),
            whole(ATTN_WIDTH, d),
            whole(sgw, d),
            whole(d, d),
        ],
        out_specs=row(d),
        out_shape=jax.ShapeDtypeStruct((n_tiles * TM, d), F32),
        compiler_params=_cparams(1),
        name="merge",
    )(xc, mods, ab, z, z, z, attn, gu, svn, gate, conv_w, w_s, bias_tab, w_a, w_b, w_c, w_o)


def _ffn_kernel(x_ref, mod_ref, n2_ref, w1_ref, w3_ref, w2_ref, o_ref, *, d, ff_chunk):
    x = x_ref[...]
    shift = mod_ref[:, 3 * d:4 * d]
    scale = mod_ref[:, 4 * d:5 * d]
    gate = mod_ref[:, 5 * d:6 * d]
    hb = ((_rms(x) * n2_ref[...]) * (1.0 + scale) + shift).astype(BF16)
    d_ff = w1_ref.shape[1]
    acc = None
    for c in range(d_ff // ff_chunk):
        cs = slice(c * ff_chunk, (c + 1) * ff_chunk)
        a = _dot(hb, w1_ref[:, cs])
        b = _dot(hb, w3_ref[:, cs])
        act = ((a * jax.nn.sigmoid(a)) * b).astype(BF16)
        part = _dot(act, w2_ref[cs, :])
        acc = part if acc is None else acc + part
    o_ref[...] = x + gate * acc


def _ffn(xc, mods, l, norm2, w1, w3, w2, geo, n_tiles):
    d = xc.shape[1]
    d_ff = w1.shape[-1]
    n_lat, n_s, bsz = geo["n_lat"], geo["n_s"], geo["B"]
    ff_chunk = d_ff // 2 if (d_ff // 2) % LANES == 0 else d_ff

    def mod_row(i):
        return jnp.where(i < n_lat, i // n_s, bsz)

    row = lambda w: pl.BlockSpec((TM, w), lambda i: (i, 0))
    return pl.pallas_call(
        functools.partial(_ffn_kernel, d=d, ff_chunk=ff_chunk),
        grid=(n_tiles,),
        in_specs=[
            row(d),
            pl.BlockSpec((None, None, 1, N_MOD * d), lambda i: (l, mod_row(i), 0, 0)),
            pl.BlockSpec((None, 1, d), lambda i: (l, 0, 0)),
            pl.BlockSpec((None, d, d_ff), lambda i: (l, 0, 0), pipeline_mode=pl.Buffered(1)),
            pl.BlockSpec((None, d, d_ff), lambda i: (l, 0, 0), pipeline_mode=pl.Buffered(1)),
            pl.BlockSpec((None, d_ff, d), lambda i: (l, 0, 0), pipeline_mode=pl.Buffered(1)),
        ],
        out_specs=row(d),
        out_shape=jax.ShapeDtypeStruct((n_tiles * TM, d), F32),
        compiler_params=_cparams(1),
        name="ffn",
    )(xc, mods, norm2, w1, w3, w2)


def _rope_table(s_len):
    t = jnp.arange(s_len, dtype=jnp.int32)
    pos = jnp.stack([(t // GRID_W).astype(F32), (t % GRID_W).astype(F32)], axis=1)
    inv_freq = ROPE_THETA ** (-jnp.arange(0, AXIS_DIM, 2, dtype=F32) / AXIS_DIM)
    ang = pos[:, :, None] * inv_freq
    cos, sin = jnp.cos(ang), jnp.sin(ang)
    zero = jnp.zeros_like(sin)
    c64 = jnp.stack([cos, cos], axis=2).reshape(s_len, HEAD_DIM)
    a64 = jnp.stack([-sin, zero], axis=2).reshape(s_len, HEAD_DIM)
    b64 = jnp.stack([zero, sin], axis=2).reshape(s_len, HEAD_DIM)
    tab = jnp.stack([c64, a64, b64], axis=0)
    ident = jnp.stack([jnp.ones((TM, HEAD_DIM), F32), jnp.zeros((TM, HEAD_DIM), F32),
                       jnp.zeros((TM, HEAD_DIM), F32)], axis=0)
    tab = jnp.concatenate([tab, ident], axis=1)
    return jnp.tile(tab, (1, 1, LANES // HEAD_DIM))


def _head_mean_matrix():
    r = jnp.arange(ATTN_WIDTH) // HEAD_DIM
    return jnp.where(r[:, None] == r[None, :], 1.0 / HEAD_DIM, 0.0).astype(BF16)


def kernel(x, c, ctx, c_ctx, w_mod, b_mod, norm1, w_in, q_gain, k_gain, conv_w, sg_norm,
           w_s, b_s, w_a, w_b, w_c, w_o, norm2, w_ff1, w_ff3, w_ff2):
    bsz, s_len, d = x.shape
    l_len = ctx.shape[1]
    depth = w_mod.shape[0]
    sgw = sg_norm.shape[1]
    assert s_len % TM == 0 and l_len % TM == 0 and s_len % TK == 0 and s_len % GRID_W == 0
    assert (bsz * s_len) % l_len == 0 and bsz + 1 <= MOD_ROWS
    n_s, n_l = s_len // TM, l_len // TM
    geo = {"B": bsz, "n_s": n_s, "n_l": n_l, "n_lat": bsz * n_s}
    n_all = bsz * (n_s + n_l)

    cond = jnp.concatenate(
        [c, c_ctx[None, :], jnp.zeros((MOD_ROWS - bsz - 1, d), F32)], axis=0)
    mods = _adaln_all(cond, w_mod, b_mod).reshape(depth, MOD_ROWS, 1, N_MOD * d)

    rope = _rope_table(s_len)
    e = _head_mean_matrix()
    qg = jnp.tile(q_gain, (1, N_Q_HEADS)).reshape(depth, 1, ATTN_WIDTH)
    kg = jnp.tile(k_gain, (1, N_KV_HEADS)).reshape(depth, 1, KV_WIDTH)
    sgn = sg_norm.reshape(depth, 1, sgw)
    n1 = norm1.reshape(depth, 1, d)
    n2 = norm2.reshape(depth, 1, d)
    bias_tab = jnp.repeat(jnp.swapaxes(b_s, 1, 2), sgw // SG_GROUPS, axis=2)
    w_in_b, w_s_b = w_in.astype(BF16), w_s.astype(BF16)
    w_a_b, w_b_b, w_c_b, w_o_b = (w.astype(BF16) for w in (w_a, w_b, w_c, w_o))
    w1_b, w3_b, w2_b = (w.astype(BF16) for w in (w_ff1, w_ff3, w_ff2))

    xc = jnp.concatenate([x.reshape(bsz * s_len, d), ctx.reshape(bsz * l_len, d)], axis=0)
    for l in range(depth):
        n_tiles = geo["n_lat"] if l == depth - 1 else n_all
        ab, z, q, kk, vv, gu, svn, gate = _proj(xc, mods, l, n1, w_in_b, qg, kg, sgn, rope, e, geo)
        attn = _attn(q, kk, vv, geo, n_tiles)
        xc = _merge(xc, mods, l, ab, z, attn, gu, svn, gate, conv_w, w_s_b, bias_tab,
                    w_a_b, w_b_b, w_c_b, w_o_b, geo, n_tiles)
        xc = _ffn(xc, mods, l, n2, w1_b, w3_b, w2_b, geo, n_tiles)
    return xc.reshape(bsz, s_len, d)
```

```python
import functools
import math

import jax
import jax.numpy as jnp
from jax import lax
from jax.experimental import pallas as pl
from jax.experimental.pallas import tpu as pltpu

F32 = jnp.float32
BF16 = jnp.bfloat16

GRID_W = 64
HEAD_DIM = 64
N_Q_HEADS = 8
N_KV_HEADS = 2
GROUP = N_Q_HEADS // N_KV_HEADS
ATTN_WIDTH = N_Q_HEADS * HEAD_DIM
KV_WIDTH = N_KV_HEADS * HEAD_DIM
AXIS_DIM = HEAD_DIM // 2
ROPE_THETA = 10000.0
CONV_K = 3
CHUNK = 128
SG_GROUPS = 4
N_BRANCH = 3
N_MOD = 6
EPS = 1e-6

LANES = 128
SUBLANES = 8
TM = 256
TK = 1024
SKEW = 3
MOD_ROWS = 8
VMEM_LIMIT = 56 * 1024 * 1024


def _cparams(n_axes=1):
    return pltpu.CompilerParams(dimension_semantics=("arbitrary",) * n_axes,
                                vmem_limit_bytes=VMEM_LIMIT)


def _dot(a, b):
    return jnp.dot(a, b, preferred_element_type=F32)


def _dot_t(a, b):
    return lax.dot_general(a, b, (((1,), (1,)), ((), ())), preferred_element_type=F32)


def _mod_kernel(cond_ref, w_ref, b_ref, o_ref):
    cnd = cond_ref[...]
    s = (cnd * jax.nn.sigmoid(cnd)).astype(BF16)
    o_ref[...] = _dot(s, w_ref[...].astype(BF16)) + b_ref[...]


def _adaln_all(cond, w_mod, b_mod):
    depth, d, width = w_mod.shape
    tn = width // 4
    return pl.pallas_call(
        _mod_kernel,
        grid=(depth, width // tn),
        in_specs=[
            pl.BlockSpec((MOD_ROWS, d), lambda l, j: (0, 0)),
            pl.BlockSpec((None, d, tn), lambda l, j: (l, 0, j)),
            pl.BlockSpec((None, 1, tn), lambda l, j: (l, 0, j)),
        ],
        out_specs=pl.BlockSpec((None, MOD_ROWS, tn), lambda l, j: (l, 0, j)),
        out_shape=jax.ShapeDtypeStruct((depth, MOD_ROWS, width), F32),
        compiler_params=_cparams(2),
        name="adaln",
    )(cond, w_mod, b_mod.reshape(depth, 1, width))


def _rms(x, eps=EPS):
    return x * lax.rsqrt(jnp.mean(x * x, axis=-1, keepdims=True) + eps)


def _gelu_tanh(x):
    c = math.sqrt(2.0 / math.pi)
    return 0.5 * x * (1.0 + jnp.tanh(c * (x + 0.044715 * (x * x * x))))


def _head_mean_sq(p, e):
    sq = p * p
    hi = sq.astype(BF16)
    lo = (sq - hi.astype(F32)).astype(BF16)
    return _dot(hi, e) + _dot(lo, e)


def _rope_slab(t, cos, sa, sb):
    up = pltpu.roll(t, LANES - 16, axis=1)
    dn = pltpu.roll(t, 16, axis=1)
    return t * cos + up * sa + dn * sb


def _proj_kernel(x_ref, mod_ref, n1_ref, w_ref, qg_ref, kg_ref, sgn_ref, rope_ref, e_ref,
                 ab_ref, z_ref, q_ref, kk_ref, vv_ref, gu_ref, svn_ref, gate_ref, *, d, cw, sgw):
    off_q = 3 * cw
    off_k = off_q + ATTN_WIDTH
    off_v = off_k + KV_WIDTH
    off_u = off_v + KV_WIDTH
    off_g = off_u + 2 * sgw

    x = x_ref[...]
    shift = mod_ref[:, 0:d]
    scale = mod_ref[:, d:2 * d]
    h = (_rms(x) * n1_ref[...]) * (1.0 + scale) + shift
    hb = h.astype(BF16)

    pc = _dot(hb, w_ref[:, 0:off_q])
    ab_ref[...] = pc[:, 0:cw]
    z_ref[...] = pc[:, cw:2 * cw] * pc[:, 2 * cw:3 * cw]

    cos = rope_ref[0]
    sa = rope_ref[1]
    sb = rope_ref[2]
    e = e_ref[...]

    pq = _dot(hb, w_ref[:, off_q:off_k])
    qn = pq * lax.rsqrt(_head_mean_sq(pq, e) + EPS) * qg_ref[...]
    for s in range(ATTN_WIDTH // LANES):
        r = _rope_slab(qn[:, s * LANES:(s + 1) * LANES], cos, sa, sb)
        q_ref[:, s * LANES:(s + 1) * LANES] = (r * (HEAD_DIM ** -0.5)).astype(BF16)

    pk = _dot(hb, w_ref[:, off_k:off_v])
    kn = pk * lax.rsqrt(_head_mean_sq(pk, e[0:KV_WIDTH, 0:KV_WIDTH]) + EPS) * kg_ref[...]
    kr = _rope_slab(kn, cos, sa, sb)
    ksw = pltpu.roll(kr, HEAD_DIM, axis=1)
    lane = lax.broadcasted_iota(jnp.int32, kr.shape, 1)
    lo_half = lane < HEAD_DIM
    zero = jnp.zeros_like(kr)
    kk_ref[:, 0 * LANES:1 * LANES] = jnp.where(lo_half, kr, zero).astype(BF16)
    kk_ref[:, 1 * LANES:2 * LANES] = jnp.where(lo_half, zero, ksw).astype(BF16)
    kk_ref[:, 2 * LANES:3 * LANES] = jnp.where(lo_half, ksw, zero).astype(BF16)
    kk_ref[:, 3 * LANES:4 * LANES] = jnp.where(lo_half, zero, kr).astype(BF16)

    pv = _dot(hb, w_ref[:, off_v:off_u])
    vsw = pltpu.roll(pv, HEAD_DIM, axis=1)
    one = jnp.ones_like(pv)
    vv_ref[:, 0:LANES] = jnp.where(lo_half, pv, one).astype(BF16)
    vv_ref[:, LANES:2 * LANES] = jnp.where(lo_half, vsw, one).astype(BF16)

    pu = _dot(hb, w_ref[:, off_u:off_g])
    gu_ref[...] = _gelu_tanh(pu[:, 0:sgw])
    svn_ref[...] = (_rms(_gelu_tanh(pu[:, sgw:2 * sgw])) * sgn_ref[...]).astype(BF16)

    for j in range(N_BRANCH):
        pg = _dot(hb, w_ref[:, off_g + j * d:off_g + (j + 1) * d])
        gate_ref[:, j * d:(j + 1) * d] = jax.nn.sigmoid(pg)


def _proj(xc, mods, l, norm1, w_in, qg, kg, sgn, rope, e, geo):
    ntok, d = xc.shape
    in_w = w_in.shape[-1]
    cw = d // 4
    sgw = d // 4
    n_lat, n_s, bsz = geo["n_lat"], geo["n_s"], geo["B"]
    n_t = ntok // TM

    def mod_row(i):
        return jnp.where(i < n_lat, i // n_s, bsz)

    def rope_blk(i):
        return jnp.where(i < n_lat, i % n_s, n_s)

    row = lambda w: pl.BlockSpec((TM, w), lambda i: (i, 0))
    outs = [(cw, F32), (cw, F32), (ATTN_WIDTH, BF16), (4 * LANES, BF16), (2 * LANES, BF16),
            (sgw, F32), (sgw, BF16), (N_BRANCH * d, F32)]
    return pl.pallas_call(
        functools.partial(_proj_kernel, d=d, cw=cw, sgw=sgw),
        grid=(n_t,),
        in_specs=[
            row(d),
            pl.BlockSpec((None, None, 1, N_MOD * d), lambda i: (l, mod_row(i), 0, 0)),
            pl.BlockSpec((None, 1, d), lambda i: (l, 0, 0)),
            pl.BlockSpec((None, d, in_w), lambda i: (l, 0, 0)),
            pl.BlockSpec((None, 1, ATTN_WIDTH), lambda i: (l, 0, 0)),
            pl.BlockSpec((None, 1, KV_WIDTH), lambda i: (l, 0, 0)),
            pl.BlockSpec((None, 1, sgw), lambda i: (l, 0, 0)),
            pl.BlockSpec((3, TM, LANES), lambda i: (0, rope_blk(i), 0)),
            pl.BlockSpec((ATTN_WIDTH, ATTN_WIDTH), lambda i: (0, 0)),
        ],
        out_specs=[row(w) for w, _ in outs],
        out_shape=[jax.ShapeDtypeStruct((ntok, w), dt) for w, dt in outs],
        compiler_params=_cparams(1),
        name="proj",
    )(xc, mods, norm1, w_in, qg, kg, sgn, rope, e)


def _attn_kernel(q_ref, kl_ref, vl_ref, kc_ref, vc_ref, o_ref, m_scr, acc_scr, *, n_lat, s_len):
    i = pl.program_id(0)
    tq = q_ref.shape[0]
    m_scr[...] = jnp.full(m_scr.shape, -jnp.inf, F32)
    acc_scr[...] = jnp.zeros(acc_scr.shape, F32)

    def step(k_tile, v_tile):
        def scores(hd):
            slab, par, kvh = hd // 2, hd % 2, hd // GROUP
            qs = q_ref[:, slab * LANES:(slab + 1) * LANES]
            kcol = (2 * kvh + par) * LANES
            return _dot_t(qs, k_tile[:, kcol:kcol + LANES])

        pending = [scores(hd) for hd in range(SKEW)]
        for hd in range(N_Q_HEADS):
            kvh = hd // GROUP
            if hd + SKEW < N_Q_HEADS:
                pending.append(scores(hd + SKEW))
            s = pending.pop(0)
            m_prev = m_scr[hd]
            m_new = jnp.maximum(m_prev, jnp.max(s, axis=-1, keepdims=True))
            alpha = jnp.exp(m_prev - m_new)
            p = jnp.exp(s - m_new[:, 0:1]).astype(BF16)
            pv = _dot(p, v_tile[:, kvh * LANES:(kvh + 1) * LANES])
            acc_scr[hd] = acc_scr[hd] * alpha + pv
            m_scr[hd] = m_new

    @pl.when(i < n_lat)
    def _():
        def body(j, carry):
            r0 = pl.multiple_of(j * TK, TK)
            step(kl_ref[pl.ds(r0, TK), :], vl_ref[pl.ds(r0, TK), :])
            return carry
        lax.fori_loop(0, s_len // TK, body, 0)

    step(kc_ref[...], vc_ref[...])

    lane = lax.broadcasted_iota(jnp.int32, (tq, LANES), 1)
    lo_half = lane < HEAD_DIM
    for slab in range(N_Q_HEADS // 2):
        ae = acc_scr[2 * slab]
        ao = acc_scr[2 * slab + 1]
        ne = ae / pltpu.roll(ae, HEAD_DIM, axis=1)
        no = ao / pltpu.roll(ao, HEAD_DIM, axis=1)
        out = jnp.where(lo_half, ne, pltpu.roll(no, HEAD_DIM, axis=1))
        o_ref[:, slab * LANES:(slab + 1) * LANES] = out.astype(o_ref.dtype)


def _attn(q, kk, vv, geo, n_tiles):
    n_lat, n_s, n_l, bsz = geo["n_lat"], geo["n_s"], geo["n_l"], geo["B"]
    s_len, l_len = n_s * TM, n_l * TM
    ctx_blk0 = (bsz * s_len) // l_len

    def lat_b(i):
        return jnp.minimum(i // n_s, bsz - 1)

    def ctx_b(i):
        return ctx_blk0 + jnp.where(i < n_lat, i // n_s, (i - n_lat) // n_l)

    return pl.pallas_call(
        functools.partial(_attn_kernel, n_lat=n_lat, s_len=s_len),
        grid=(n_tiles,),
        in_specs=[
            pl.BlockSpec((TM, ATTN_WIDTH), lambda i: (i, 0)),
            pl.BlockSpec((s_len, 4 * LANES), lambda i: (lat_b(i), 0)),
            pl.BlockSpec((s_len, 2 * LANES), lambda i: (lat_b(i), 0)),
            pl.BlockSpec((l_len, 4 * LANES), lambda i: (ctx_b(i), 0)),
            pl.BlockSpec((l_len, 2 * LANES), lambda i: (ctx_b(i), 0)),
        ],
        out_specs=pl.BlockSpec((TM, ATTN_WIDTH), lambda i: (i, 0)),
        out_shape=jax.ShapeDtypeStruct((n_tiles * TM, ATTN_WIDTH), BF16),
        scratch_shapes=[pltpu.VMEM((N_Q_HEADS, TM, LANES), F32),
                        pltpu.VMEM((N_Q_HEADS, TM, LANES), F32)],
        compiler_params=_cparams(1),
        name="attn",
    )(q, kk, vv, kk, vv)


def _merge_kernel(x_ref, mod_ref, ab_ref, z_ref, zp_ref, zn_ref, at_ref, gu_ref, svn_ref, gate_ref,
                  cw_ref, ws_ref, bt_ref, wa_ref, wb_ref, wc_ref, wo_ref, o_ref,
                  *, d, n_lat, n_s, n_l):
    i = pl.program_id(0)
    tm = x_ref.shape[0]
    t_in_seq = jnp.where(i < n_lat, i % n_s, (i - n_lat) % n_l)
    seq_tiles = jnp.where(i < n_lat, n_s, n_l)
    has_prev = (t_in_seq > 0).astype(F32)
    has_next = (t_in_seq < seq_tiles - 1).astype(F32)

    z = z_ref[...]
    row = lax.broadcasted_iota(jnp.int32, z.shape, 0)
    prev_row = zp_ref[SUBLANES - 1:SUBLANES, :] * has_prev
    next_row = zn_ref[0:1, :] * has_next
    zp = jnp.where(row == 0, prev_row, pltpu.roll(z, 1, axis=0))
    zn = jnp.where(row == tm - 1, next_row, pltpu.roll(z, tm - 1, axis=0))
    conv = zp * cw_ref[0:1, :] + z * cw_ref[1:2, :] + zn * cw_ref[2:3, :]
    ya = _dot((ab_ref[...] * conv).astype(BF16), wa_ref[...])

    yb = _dot(at_ref[...], wb_ref[...])

    sgw = svn_ref.shape[1]
    gw = sgw // SG_GROUPS
    lane = lax.broadcasted_iota(jnp.int32, (CHUNK, sgw), 1)
    parts = []
    for c in range(tm // CHUNK):
        vc = svn_ref[c * CHUNK:(c + 1) * CHUNK, :]
        mixed = bt_ref[...]
        for g in range(SG_GROUPS):
            in_g = (lane >= g * gw) & (lane < (g + 1) * gw)
            mixed = mixed + _dot(ws_ref[g], jnp.where(in_g, vc, jnp.zeros_like(vc)))
        parts.append(gu_ref[c * CHUNK:(c + 1) * CHUNK, :] * mixed)
    yc = _dot(jnp.concatenate(parts, axis=0).astype(BF16), wc_ref[...])

    y = gate_ref[:, 0:d] * ya + gate_ref[:, d:2 * d] * yb + gate_ref[:, 2 * d:3 * d] * yc
    out = _dot(y.astype(BF16), wo_ref[...])
    o_ref[...] = x_ref[...] + mod_ref[:, 2 * d:3 * d] * out


def _merge(xc, mods, l, ab, z, attn, gu, svn, gate, conv_w, w_s, bias_tab, w_a, w_b, w_c, w_o,
           geo, n_tiles):
    d = xc.shape[1]
    cw = ab.shape[1]
    sgw = gu.shape[1]
    n_lat, n_s, n_l, bsz = geo["n_lat"], geo["n_s"], geo["n_l"], geo["B"]
    rows_per_tile = TM // SUBLANES
    last_blk = z.shape[0] // SUBLANES - 1

    def mod_row(i):
        return jnp.where(i < n_lat, i // n_s, bsz)

    row = lambda w: pl.BlockSpec((TM, w), lambda i: (i, 0))
    whole = lambda a, b: pl.BlockSpec((None, a, b), lambda i: (l, 0, 0))
    return pl.pallas_call(
        functools.partial(_merge_kernel, d=d, n_lat=n_lat, n_s=n_s, n_l=n_l),
        grid=(n_tiles,),
        in_specs=[
            row(d),
            pl.BlockSpec((None, None, 1, N_MOD * d), lambda i: (l, mod_row(i), 0, 0)),
            row(cw),
            row(cw),
            pl.BlockSpec((SUBLANES, cw), lambda i: (jnp.maximum(i * rows_per_tile - 1, 0), 0)),
            pl.BlockSpec((SUBLANES, cw),
                         lambda i: (jnp.minimum((i + 1) * rows_per_tile, last_blk), 0)),
            row(ATTN_WIDTH),
            row(sgw),
            row(sgw),
            row(N_BRANCH * d),
            whole(CONV_K, cw),
            pl.BlockSpec((None, SG_GROUPS, CHUNK, CHUNK), lambda i: (l, 0, 0, 0)),
            whole(CHUNK, sgw),
            whole(cw, d),
            whole(ATTN_WIDTH, d),
            whole(sgw, d),
            whole(d, d),
        ],
        out_specs=row(d),
        out_shape=jax.ShapeDtypeStruct((n_tiles * TM, d), F32),
        compiler_params=_cparams(1),
        name="merge",
    )(xc, mods, ab, z, z, z, attn, gu, svn, gate, conv_w, w_s, bias_tab, w_a, w_b, w_c, w_o)


def _ffn_kernel(x_ref, mod_ref, n2_ref, w1_ref, w3_ref, w2_ref, o_ref, *, d, ff_chunk):
    x = x_ref[...]
    shift = mod_ref[:, 3 * d:4 * d]
    scale = mod_ref[:, 4 * d:5 * d]
    gate = mod_ref[:, 5 * d:6 * d]
    hb = ((_rms(x) * n2_ref[...]) * (1.0 + scale) + shift).astype(BF16)
    d_ff = w1_ref.shape[1]
    acc = None
    for c in range(d_ff // ff_chunk):
        cs = slice(c * ff_chunk, (c + 1) * ff_chunk)
        a = _dot(hb, w1_ref[:, cs])
        b = _dot(hb, w3_ref[:, cs])
        act = ((a * jax.nn.sigmoid(a)) * b).astype(BF16)
        part = _dot(act, w2_ref[cs, :])
        acc = part if acc is None else acc + part
    o_ref[...] = x + gate * acc


def _ffn(xc, mods, l, norm2, w1, w3, w2, geo, n_tiles):
    d = xc.shape[1]
    d_ff = w1.shape[-1]
    n_lat, n_s, bsz = geo["n_lat"], geo["n_s"], geo["B"]
    ff_chunk = d_ff // 2 if (d_ff // 2) % LANES == 0 else d_ff

    def mod_row(i):
        return jnp.where(i < n_lat, i // n_s, bsz)

    row = lambda w: pl.BlockSpec((TM, w), lambda i: (i, 0))
    return pl.pallas_call(
        functools.partial(_ffn_kernel, d=d, ff_chunk=ff_chunk),
        grid=(n_tiles,),
        in_specs=[
            row(d),
            pl.BlockSpec((None, None, 1, N_MOD * d), lambda i: (l, mod_row(i), 0, 0)),
            pl.BlockSpec((None, 1, d), lambda i: (l, 0, 0)),
            pl.BlockSpec((None, d, d_ff), lambda i: (l, 0, 0), pipeline_mode=pl.Buffered(1)),
            pl.BlockSpec((None, d, d_ff), lambda i: (l, 0, 0), pipeline_mode=pl.Buffered(1)),
            pl.BlockSpec((None, d_ff, d), lambda i: (l, 0, 0), pipeline_mode=pl.Buffered(1)),
        ],
        out_specs=row(d),
        out_shape=jax.ShapeDtypeStruct((n_tiles * TM, d), F32),
        compiler_params=_cparams(1),
        name="ffn",
    )(xc, mods, norm2, w1, w3, w2)


def _rope_table(s_len):
    t = jnp.arange(s_len, dtype=jnp.int32)
    pos = jnp.stack([(t // GRID_W).astype(F32), (t % GRID_W).astype(F32)], axis=1)
    inv_freq = ROPE_THETA ** (-jnp.arange(0, AXIS_DIM, 2, dtype=F32) / AXIS_DIM)
    ang = pos[:, :, None] * inv_freq
    cos, sin = jnp.cos(ang), jnp.sin(ang)
    zero = jnp.zeros_like(sin)
    c64 = jnp.stack([cos, cos], axis=2).reshape(s_len, HEAD_DIM)
    a64 = jnp.stack([-sin, zero], axis=2).reshape(s_len, HEAD_DIM)
    b64 = jnp.stack([zero, sin], axis=2).reshape(s_len, HEAD_DIM)
    tab = jnp.stack([c64, a64, b64], axis=0)
    ident = jnp.stack([jnp.ones((TM, HEAD_DIM), F32), jnp.zeros((TM, HEAD_DIM), F32),
                       jnp.zeros((TM, HEAD_DIM), F32)], axis=0)
    tab = jnp.concatenate([tab, ident], axis=1)
    return jnp.tile(tab, (1, 1, LANES // HEAD_DIM))


def _head_mean_matrix():
    r = jnp.arange(ATTN_WIDTH) // HEAD_DIM
    return jnp.where(r[:, None] == r[None, :], 1.0 / HEAD_DIM, 0.0).astype(BF16)


def kernel(x, c, ctx, c_ctx, w_mod, b_mod, norm1, w_in, q_gain, k_gain, conv_w, sg_norm,
           w_s, b_s, w_a, w_b, w_c, w_o, norm2, w_ff1, w_ff3, w_ff2):
    bsz, s_len, d = x.shape
    l_len = ctx.shape[1]
    depth = w_mod.shape[0]
    sgw = sg_norm.shape[1]
    assert s_len % TM == 0 and l_len % TM == 0 and s_len % TK == 0 and s_len % GRID_W == 0
    assert (bsz * s_len) % l_len == 0 and bsz + 1 <= MOD_ROWS
    n_s, n_l = s_len // TM, l_len // TM
    geo = {"B": bsz, "n_s": n_s, "n_l": n_l, "n_lat": bsz * n_s}
    n_all = bsz * (n_s + n_l)

    cond = jnp.concatenate(
        [c, c_ctx[None, :], jnp.zeros((MOD_ROWS - bsz - 1, d), F32)], axis=0)
    mods = _adaln_all(cond, w_mod, b_mod).reshape(depth, MOD_ROWS, 1, N_MOD * d)

    rope = _rope_table(s_len)
    e = _head_mean_matrix()
    qg = jnp.tile(q_gain, (1, N_Q_HEADS)).reshape(depth, 1, ATTN_WIDTH)
    kg = jnp.tile(k_gain, (1, N_KV_HEADS)).reshape(depth, 1, KV_WIDTH)
    sgn = sg_norm.reshape(depth, 1, sgw)
    n1 = norm1.reshape(depth, 1, d)
    n2 = norm2.reshape(depth, 1, d)
    bias_tab = jnp.repeat(jnp.swapaxes(b_s, 1, 2), sgw // SG_GROUPS, axis=2)
    w_in_b, w_s_b = w_in.astype(BF16), w_s.astype(BF16)
    w_a_b, w_b_b, w_c_b, w_o_b = (w.astype(BF16) for w in (w_a, w_b, w_c, w_o))
    w1_b, w3_b, w2_b = (w.astype(BF16) for w in (w_ff1, w_ff3, w_ff2))

    xc = jnp.concatenate([x.reshape(bsz * s_len, d), ctx.reshape(bsz * l_len, d)], axis=0)
    for l in range(depth):
        n_tiles = geo["n_lat"] if l == depth - 1 else n_all
        ab, z, q, kk, vv, gu, svn, gate = _proj(xc, mods, l, n1, w_in_b, qg, kg, sgn, rope, e, geo)
        attn = _attn(q, kk, vv, geo, n_tiles)
        xc = _merge(xc, mods, l, ab, z, attn, gu, svn, gate, conv_w, w_s_b, bias_tab,
                    w_a_b, w_b_b, w_c_b, w_o_b, geo, n_tiles)
        xc = _ffn(xc, mods, l, n2, w1_b, w3_b, w2_b, geo, n_tiles)
    return xc.reshape(bsz, s_len, d)
```

```python
import functools
import math

import jax
import jax.numpy as jnp
from jax import lax
from jax.experimental import pallas as pl
from jax.experimental.pallas import tpu as pltpu

F32 = jnp.float32
BF16 = jnp.bfloat16

GRID_W = 64
HEAD_DIM = 64
N_Q_HEADS = 8
N_KV_HEADS = 2
GROUP = N_Q_HEADS // N_KV_HEADS
ATTN_WIDTH = N_Q_HEADS * HEAD_DIM
KV_WIDTH = N_KV_HEADS * HEAD_DIM
AXIS_DIM = HEAD_DIM // 2
ROPE_THETA = 10000.0
CONV_K = 3
CHUNK = 128
SG_GROUPS = 4
N_BRANCH = 3
N_MOD = 6
EPS = 1e-6

LANES = 128
SUBLANES = 8
TM = 256
TK = 512
ONES_ROWS = 16
VT_ROWS = HEAD_DIM + ONES_ROWS
Q_SCALE = HEAD_DIM ** -0.5 * math.log2(math.e)
SKEW = 2
MOD_ROWS = 8
VMEM_LIMIT = 56 * 1024 * 1024


def _cparams(n_axes=1):
    return pltpu.CompilerParams(dimension_semantics=("arbitrary",) * n_axes,
                                vmem_limit_bytes=VMEM_LIMIT)


def _dot(a, b):
    return jnp.dot(a, b, preferred_element_type=F32)


def _dot_t(a, b):
    return lax.dot_general(a, b, (((1,), (1,)), ((), ())), preferred_element_type=F32)


def _mod_kernel(cond_ref, w_ref, b_ref, o_ref):
    cnd = cond_ref[...]
    s = (cnd * jax.nn.sigmoid(cnd)).astype(BF16)
    o_ref[...] = _dot(s, w_ref[...].astype(BF16)) + b_ref[...]


def _adaln_all(cond, w_mod, b_mod):
    depth, d, width = w_mod.shape
    tn = width // 4
    return pl.pallas_call(
        _mod_kernel,
        grid=(depth, width // tn),
        in_specs=[
            pl.BlockSpec((MOD_ROWS, d), lambda l, j: (0, 0)),
            pl.BlockSpec((None, d, tn), lambda l, j: (l, 0, j)),
            pl.BlockSpec((None, 1, tn), lambda l, j: (l, 0, j)),
        ],
        out_specs=pl.BlockSpec((None, MOD_ROWS, tn), lambda l, j: (l, 0, j)),
        out_shape=jax.ShapeDtypeStruct((depth, MOD_ROWS, width), F32),
        compiler_params=_cparams(2),
        name="adaln",
    )(cond, w_mod, b_mod.reshape(depth, 1, width))


def _rms(x, eps=EPS):
    return x * lax.rsqrt(jnp.mean(x * x, axis=-1, keepdims=True) + eps)


def _gelu_tanh(x):
    c = math.sqrt(2.0 / math.pi)
    return 0.5 * x * (1.0 + jnp.tanh(c * (x + 0.044715 * (x * x * x))))


def _head_mean_sq(p, e):
    sq = p * p
    hi = sq.astype(BF16)
    lo = (sq - hi.astype(F32)).astype(BF16)
    return _dot(hi, e) + _dot(lo, e)


def _rope_slab(t, cos, sa, sb):
    up = pltpu.roll(t, LANES - 16, axis=1)
    dn = pltpu.roll(t, 16, axis=1)
    return t * cos + up * sa + dn * sb


def _proj_kernel(x_ref, mod_ref, n1_ref, w_ref, qg_ref, kg_ref, sgn_ref, rope_ref, e_ref,
                 ab_ref, z_ref, q_ref, kk_ref, vt_ref, gu_ref, svn_ref, gate_ref, *, d, cw, sgw):
    off_q = 3 * cw
    off_k = off_q + ATTN_WIDTH
    off_v = off_k + KV_WIDTH
    off_u = off_v + KV_WIDTH
    off_g = off_u + 2 * sgw

    x = x_ref[...]
    shift = mod_ref[:, 0:d]
    scale = mod_ref[:, d:2 * d]
    h = (_rms(x) * n1_ref[...]) * (1.0 + scale) + shift
    hb = h.astype(BF16)

    pc = _dot(hb, w_ref[:, 0:off_q])
    ab_ref[...] = pc[:, 0:cw]
    z_ref[...] = pc[:, cw:2 * cw] * pc[:, 2 * cw:3 * cw]

    cos = rope_ref[0]
    sa = rope_ref[1]
    sb = rope_ref[2]
    e = e_ref[...]

    pq = _dot(hb, w_ref[:, off_q:off_k])
    qn = pq * lax.rsqrt(_head_mean_sq(pq, e) + EPS) * qg_ref[...]
    for s in range(ATTN_WIDTH // LANES):
        r = _rope_slab(qn[:, s * LANES:(s + 1) * LANES], cos, sa, sb)
        q_ref[:, s * LANES:(s + 1) * LANES] = (r * Q_SCALE).astype(BF16)

    pk = _dot(hb, w_ref[:, off_k:off_v])
    kn = pk * lax.rsqrt(_head_mean_sq(pk, e[0:KV_WIDTH, 0:KV_WIDTH]) + EPS) * kg_ref[...]
    kr = _rope_slab(kn, cos, sa, sb)
    ksw = pltpu.roll(kr, HEAD_DIM, axis=1)
    lane = lax.broadcasted_iota(jnp.int32, kr.shape, 1)
    lo_half = lane < HEAD_DIM
    zero = jnp.zeros_like(kr)
    kk_ref[:, 0 * LANES:1 * LANES] = jnp.where(lo_half, kr, zero).astype(BF16)
    kk_ref[:, 1 * LANES:2 * LANES] = jnp.where(lo_half, zero, ksw).astype(BF16)
    kk_ref[:, 2 * LANES:3 * LANES] = jnp.where(lo_half, ksw, zero).astype(BF16)
    kk_ref[:, 3 * LANES:4 * LANES] = jnp.where(lo_half, zero, kr).astype(BF16)

    pvt = _dot(hb, w_ref[:, off_v:off_u]).T.astype(BF16)
    ones = jnp.ones((ONES_ROWS, pvt.shape[1]), BF16)
    for hh in range(N_KV_HEADS):
        vt_ref[hh * VT_ROWS:hh * VT_ROWS + HEAD_DIM, :] = pvt[hh * HEAD_DIM:(hh + 1) * HEAD_DIM, :]
        vt_ref[hh * VT_ROWS + HEAD_DIM:(hh + 1) * VT_ROWS, :] = ones

    pu = _dot(hb, w_ref[:, off_u:off_g])
    gu_ref[...] = _gelu_tanh(pu[:, 0:sgw])
    svn_ref[...] = (_rms(_gelu_tanh(pu[:, sgw:2 * sgw])) * sgn_ref[...]).astype(BF16)

    for j in range(N_BRANCH):
        pg = _dot(hb, w_ref[:, off_g + j * d:off_g + (j + 1) * d])
        gate_ref[:, j * d:(j + 1) * d] = jax.nn.sigmoid(pg)


def _proj(xc, mods, l, norm1, w_in, qg, kg, sgn, rope, e, geo):
    ntok, d = xc.shape
    in_w = w_in.shape[-1]
    cw = d // 4
    sgw = d // 4
    n_lat, n_s, bsz = geo["n_lat"], geo["n_s"], geo["B"]
    n_t = ntok // TM

    def mod_row(i):
        return jnp.where(i < n_lat, i // n_s, bsz)

    def rope_blk(i):
        return jnp.where(i < n_lat, i % n_s, n_s)

    row = lambda w: pl.BlockSpec((TM, w), lambda i: (i, 0))
    outs = [(cw, F32), (cw, F32), (ATTN_WIDTH, BF16), (4 * LANES, BF16), None,
            (sgw, F32), (sgw, BF16), (N_BRANCH * d, F32)]
    vt_rows = N_KV_HEADS * VT_ROWS
    out_specs = [pl.BlockSpec((vt_rows, TM), lambda i: (0, i)) if o is None else row(o[0])
                 for o in outs]
    out_shape = [jax.ShapeDtypeStruct((vt_rows, ntok), BF16) if o is None
                 else jax.ShapeDtypeStruct((ntok, o[0]), o[1]) for o in outs]
    return pl.pallas_call(
        functools.partial(_proj_kernel, d=d, cw=cw, sgw=sgw),
        grid=(n_t,),
        in_specs=[
            row(d),
            pl.BlockSpec((None, None, 1, N_MOD * d), lambda i: (l, mod_row(i), 0, 0)),
            pl.BlockSpec((None, 1, d), lambda i: (l, 0, 0)),
            pl.BlockSpec((None, d, in_w), lambda i: (l, 0, 0)),
            pl.BlockSpec((None, 1, ATTN_WIDTH), lambda i: (l, 0, 0)),
            pl.BlockSpec((None, 1, KV_WIDTH), lambda i: (l, 0, 0)),
            pl.BlockSpec((None, 1, sgw), lambda i: (l, 0, 0)),
            pl.BlockSpec((3, TM, LANES), lambda i: (0, rope_blk(i), 0)),
            pl.BlockSpec((ATTN_WIDTH, ATTN_WIDTH), lambda i: (0, 0)),
        ],
        out_specs=out_specs,
        out_shape=out_shape,
        compiler_params=_cparams(1),
        name="proj",
    )(xc, mods, norm1, w_in, qg, kg, sgn, rope, e)


def _attn_kernel(q_ref, kl_ref, vl_ref, kc_ref, vc_ref, o_ref, m_scr, acc_scr, sa_scr, sb_scr, sc_scr,
                 *, n_lat, s_len):
    i = pl.program_id(0)
    n_steps = s_len // TK
    m_scr[...] = jnp.full(m_scr.shape, -jnp.inf, F32)
    acc_scr[...] = jnp.zeros(acc_scr.shape, F32)

    def scores(k_tile, hd):
        slab, par, kvh = hd // 2, hd % 2, hd // GROUP
        qs = q_ref[:, slab * LANES:(slab + 1) * LANES]
        kcol = (2 * kvh + par) * LANES
        return _dot_t(k_tile[:, kcol:kcol + LANES], qs)

    def consume(s, hd, vt_tile):
        kvh = hd // GROUP
        m_prev = m_scr[hd]
        m_new = jnp.maximum(m_prev, jnp.max(s, axis=0, keepdims=True))
        alpha = jnp.exp2(m_prev - m_new)
        p = jnp.exp2(s - m_new).astype(BF16)
        pv = _dot(vt_tile[kvh * VT_ROWS:(kvh + 1) * VT_ROWS, :], p)
        acc_scr[hd] = acc_scr[hd] * alpha + pv
        m_scr[hd] = m_new

    def step(cur_scr, vt_tile, next_k, next_scr):
        if next_k is not None:
            for hd in range(SKEW):
                next_scr[hd] = scores(next_k, hd)
        for hd in range(N_Q_HEADS):
            if next_k is not None and hd + SKEW < N_Q_HEADS:
                next_scr[hd + SKEW] = scores(next_k, hd + SKEW)
            consume(cur_scr[hd], hd, vt_tile)

    def k_at(j):
        return kl_ref[pl.ds(pl.multiple_of(j * TK, TK), TK), :]

    def v_at(j):
        return vl_ref[:, pl.ds(pl.multiple_of(j * TK, TK), TK)]

    @pl.when(i < n_lat)
    def _():
        for hd in range(N_Q_HEADS):
            sa_scr[hd] = scores(k_at(0), hd)

        def body(jj, carry):
            j = 2 * jj
            step(sa_scr, v_at(j), k_at(j + 1), sb_scr)
            step(sb_scr, v_at(j + 1), k_at(j + 2), sa_scr)
            return carry
        lax.fori_loop(0, n_steps // 2 - 1, body, 0)
        step(sa_scr, v_at(n_steps - 2), k_at(n_steps - 1), sb_scr)
        step(sb_scr, v_at(n_steps - 1), kc_ref[...], sc_scr)

    @pl.when(i >= n_lat)
    def _():
        for hd in range(N_Q_HEADS):
            sc_scr[hd] = scores(kc_ref[...], hd)

    step(sc_scr, vc_ref[...], None, None)

    for slab in range(N_Q_HEADS // 2):
        halves = []
        for hd in (2 * slab, 2 * slab + 1):
            a = acc_scr[hd]
            halves.append(a[0:HEAD_DIM, :] / a[HEAD_DIM:HEAD_DIM + 1, :])
        out = jnp.concatenate(halves, axis=0).T
        o_ref[:, slab * LANES:(slab + 1) * LANES] = out.astype(o_ref.dtype)


def _attn(q, kk, vt, geo, n_tiles):
    n_lat, n_s, n_l, bsz = geo["n_lat"], geo["n_s"], geo["n_l"], geo["B"]
    s_len, l_len = n_s * TM, n_l * TM
    ctx_blk0 = (bsz * s_len) // l_len
    vt_rows = N_KV_HEADS * VT_ROWS

    def lat_b(i):
        return jnp.minimum(i // n_s, bsz - 1)

    def ctx_b(i):
        return ctx_blk0 + jnp.where(i < n_lat, i // n_s, (i - n_lat) // n_l)

    return pl.pallas_call(
        functools.partial(_attn_kernel, n_lat=n_lat, s_len=s_len),
        grid=(n_tiles,),
        in_specs=[
            pl.BlockSpec((TM, ATTN_WIDTH), lambda i: (i, 0)),
            pl.BlockSpec((s_len, 4 * LANES), lambda i: (lat_b(i), 0)),
            pl.BlockSpec((vt_rows, s_len), lambda i: (0, lat_b(i))),
            pl.BlockSpec((l_len, 4 * LANES), lambda i: (ctx_b(i), 0)),
            pl.BlockSpec((vt_rows, l_len), lambda i: (0, ctx_b(i))),
        ],
        out_specs=pl.BlockSpec((TM, ATTN_WIDTH), lambda i: (i, 0)),
        out_shape=jax.ShapeDtypeStruct((n_tiles * TM, ATTN_WIDTH), BF16),
        scratch_shapes=[pltpu.VMEM((N_Q_HEADS, 1, TM), F32),
                        pltpu.VMEM((N_Q_HEADS, VT_ROWS, TM), F32),
                        pltpu.VMEM((N_Q_HEADS, TK, TM), F32),
                        pltpu.VMEM((N_Q_HEADS, TK, TM), F32),
                        pltpu.VMEM((N_Q_HEADS, l_len, TM), F32)],
        compiler_params=_cparams(1),
        name="attn",
    )(q, kk, vt, kk, vt)


def _merge_kernel(x_ref, mod_ref, ab_ref, z_ref, zp_ref, zn_ref, at_ref, gu_ref, svn_ref, gate_ref,
                  cw_ref, ws_ref, bt_ref, wa_ref, wb_ref, wc_ref, wo_ref, o_ref,
                  *, d, n_lat, n_s, n_l):
    i = pl.program_id(0)
    tm = x_ref.shape[0]
    t_in_seq = jnp.where(i < n_lat, i % n_s, (i - n_lat) % n_l)
    seq_tiles = jnp.where(i < n_lat, n_s, n_l)
    has_prev = (t_in_seq > 0).astype(F32)
    has_next = (t_in_seq < seq_tiles - 1).astype(F32)

    z = z_ref[...]
    row = lax.broadcasted_iota(jnp.int32, z.shape, 0)
    prev_row = zp_ref[SUBLANES - 1:SUBLANES, :] * has_prev
    next_row = zn_ref[0:1, :] * has_next
    zp = jnp.where(row == 0, prev_row, pltpu.roll(z, 1, axis=0))
    zn = jnp.where(row == tm - 1, next_row, pltpu.roll(z, tm - 1, axis=0))
    conv = zp * cw_ref[0:1, :] + z * cw_ref[1:2, :] + zn * cw_ref[2:3, :]
    ya = _dot((ab_ref[...] * conv).astype(BF16), wa_ref[...])

    yb = _dot(at_ref[...], wb_ref[...])

    sgw = svn_ref.shape[1]
    gw = sgw // SG_GROUPS
    lane = lax.broadcasted_iota(jnp.int32, (CHUNK, sgw), 1)
    parts = []
    for c in range(tm // CHUNK):
        vc = svn_ref[c * CHUNK:(c + 1) * CHUNK, :]
        mixed = bt_ref[...]
        for g in range(SG_GROUPS):
            in_g = (lane >= g * gw) & (lane < (g + 1) * gw)
            mixed = mixed + _dot(ws_ref[g], jnp.where(in_g, vc, jnp.zeros_like(vc)))
        parts.append(gu_ref[c * CHUNK:(c + 1) * CHUNK, :] * mixed)
    yc = _dot(jnp.concatenate(parts, axis=0).astype(BF16), wc_ref[...])

    y = gate_ref[:, 0:d] * ya + gate_ref[:, d:2 * d] * yb + gate_ref[:, 2 * d:3 * d] * yc
    out = _dot(y.astype(BF16), wo_ref[...])
    o_ref[...] = x_ref[...] + mod_ref[:, 2 * d:3 * d] * out


def _merge(xc, mods, l, ab, z, attn, gu, svn, gate, conv_w, w_s, bias_tab, w_a, w_b, w_c, w_o,
           geo, n_tiles):
    d = xc.shape[1]
    cw = ab.shape[1]
    sgw = gu.shape[1]
    n_lat, n_s, n_l, bsz = geo["n_lat"], geo["n_s"], geo["n_l"], geo["B"]
    rows_per_tile = TM // SUBLANES
    last_blk = z.shape[0] // SUBLANES - 1

    def mod_row(i):
        return jnp.where(i < n_lat, i // n_s, bsz)

    row = lambda w: pl.BlockSpec((TM, w), lambda i: (i, 0))
    whole = lambda a, b: pl.BlockSpec((None, a, b), lambda i: (l, 0, 0))
    return pl.pallas_call(
        functools.partial(_merge_kernel, d=d, n_lat=n_lat, n_s=n_s, n_l=n_l),
        grid=(n_tiles,),
        in_specs=[
            row(d),
            pl.BlockSpec((None, None, 1, N_MOD * d), lambda i: (l, mod_row(i), 0, 0)),
            row(cw),
            row(cw),
            pl.BlockSpec((SUBLANES, cw), lambda i: (jnp.maximum(i * rows_per_tile - 1, 0), 0)),
            pl.BlockSpec((SUBLANES, cw),
                         lambda i: (jnp.minimum((i + 1) * rows_per_tile, last_blk), 0)),
            row(ATTN_WIDTH),
            row(sgw),
            row(sgw),
            row(N_BRANCH * d),
            whole(CONV_K, cw),
            pl.BlockSpec((None, SG_GROUPS, CHUNK, CHUNK), lambda i: (l, 0, 0, 0)),
            whole(CHUNK, sgw),
            whole(cw, d),
            whole(ATTN_WIDTH, d),
            whole(sgw, d),
            whole(d, d),
        ],
        out_specs=row(d),
        out_shape=jax.ShapeDtypeStruct((n_tiles * TM, d), F32),
        compiler_params=_cparams(1),
        name="merge",
    )(xc, mods, ab, z, z, z, attn, gu, svn, gate, conv_w, w_s, bias_tab, w_a, w_b, w_c, w_o)


def _ffn_kernel(x_ref, mod_ref, n2_ref, w1_ref, w3_ref, w2_ref, o_ref, *, d, ff_chunk):
    x = x_ref[...]
    shift = mod_ref[:, 3 * d:4 * d]
    scale = mod_ref[:, 4 * d:5 * d]
    gate = mod_ref[:, 5 * d:6 * d]
    hb = ((_rms(x) * n2_ref[...]) * (1.0 + scale) + shift).astype(BF16)
    d_ff = w1_ref.shape[1]
    acc = None
    for c in range(d_ff // ff_chunk):
        cs = slice(c * ff_chunk, (c + 1) * ff_chunk)
        a = _dot(hb, w1_ref[:, cs])
        b = _dot(hb, w3_ref[:, cs])
        act = ((a * jax.nn.sigmoid(a)) * b).astype(BF16)
        part = _dot(act, w2_ref[cs, :])
        acc = part if acc is None else acc + part
    o_ref[...] = x + gate * acc


def _ffn(xc, mods, l, norm2, w1, w3, w2, geo, n_tiles):
    d = xc.shape[1]
    d_ff = w1.shape[-1]
    n_lat, n_s, bsz = geo["n_lat"], geo["n_s"], geo["B"]
    ff_chunk = d_ff // 2 if (d_ff // 2) % LANES == 0 else d_ff

    def mod_row(i):
        return jnp.where(i < n_lat, i // n_s, bsz)

    row = lambda w: pl.BlockSpec((TM, w), lambda i: (i, 0))
    return pl.pallas_call(
        functools.partial(_ffn_kernel, d=d, ff_chunk=ff_chunk),
        grid=(n_tiles,),
        in_specs=[
            row(d),
            pl.BlockSpec((None, None, 1, N_MOD * d), lambda i: (l, mod_row(i), 0, 0)),
            pl.BlockSpec((None, 1, d), lambda i: (l, 0, 0)),
            pl.BlockSpec((None, d, d_ff), lambda i: (l, 0, 0), pipeline_mode=pl.Buffered(1)),
            pl.BlockSpec((None, d, d_ff), lambda i: (l, 0, 0), pipeline_mode=pl.Buffered(1)),
            pl.BlockSpec((None, d_ff, d), lambda i: (l, 0, 0), pipeline_mode=pl.Buffered(1)),
        ],
        out_specs=row(d),
        out_shape=jax.ShapeDtypeStruct((n_tiles * TM, d), F32),
        compiler_params=_cparams(1),
        name="ffn",
    )(xc, mods, norm2, w1, w3, w2)


def _rope_table(s_len):
    t = jnp.arange(s_len, dtype=jnp.int32)
    pos = jnp.stack([(t // GRID_W).astype(F32), (t % GRID_W).astype(F32)], axis=1)
    inv_freq = ROPE_THETA ** (-jnp.arange(0, AXIS_DIM, 2, dtype=F32) / AXIS_DIM)
    ang = pos[:, :, None] * inv_freq
    cos, sin = jnp.cos(ang), jnp.sin(ang)
    zero = jnp.zeros_like(sin)
    c64 = jnp.stack([cos, cos], axis=2).reshape(s_len, HEAD_DIM)
    a64 = jnp.stack([-sin, zero], axis=2).reshape(s_len, HEAD_DIM)
    b64 = jnp.stack([zero, sin], axis=2).reshape(s_len, HEAD_DIM)
    tab = jnp.stack([c64, a64, b64], axis=0)
    ident = jnp.stack([jnp.ones((TM, HEAD_DIM), F32), jnp.zeros((TM, HEAD_DIM), F32),
                       jnp.zeros((TM, HEAD_DIM), F32)], axis=0)
    tab = jnp.concatenate([tab, ident], axis=1)
    return jnp.tile(tab, (1, 1, LANES // HEAD_DIM))


def _head_mean_matrix():
    r = jnp.arange(ATTN_WIDTH) // HEAD_DIM
    return jnp.where(r[:, None] == r[None, :], 1.0 / HEAD_DIM, 0.0).astype(BF16)


def kernel(x, c, ctx, c_ctx, w_mod, b_mod, norm1, w_in, q_gain, k_gain, conv_w, sg_norm,
           w_s, b_s, w_a, w_b, w_c, w_o, norm2, w_ff1, w_ff3, w_ff2):
    bsz, s_len, d = x.shape
    l_len = ctx.shape[1]
    depth = w_mod.shape[0]
    sgw = sg_norm.shape[1]
    assert s_len % TM == 0 and l_len % TM == 0 and s_len % (2 * TK) == 0 and s_len % GRID_W == 0
    assert (bsz * s_len) % l_len == 0 and bsz + 1 <= MOD_ROWS
    n_s, n_l = s_len // TM, l_len // TM
    geo = {"B": bsz, "n_s": n_s, "n_l": n_l, "n_lat": bsz * n_s}
    n_all = bsz * (n_s + n_l)

    cond = jnp.concatenate(
        [c, c_ctx[None, :], jnp.zeros((MOD_ROWS - bsz - 1, d), F32)], axis=0)
    mods = _adaln_all(cond, w_mod, b_mod).reshape(depth, MOD_ROWS, 1, N_MOD * d)

    rope = _rope_table(s_len)
    e = _head_mean_matrix()
    qg = jnp.tile(q_gain, (1, N_Q_HEADS)).reshape(depth, 1, ATTN_WIDTH)
    kg = jnp.tile(k_gain, (1, N_KV_HEADS)).reshape(depth, 1, KV_WIDTH)
    sgn = sg_norm.reshape(depth, 1, sgw)
    n1 = norm1.reshape(depth, 1, d)
    n2 = norm2.reshape(depth, 1, d)
    bias_tab = jnp.repeat(jnp.swapaxes(b_s, 1, 2), sgw // SG_GROUPS, axis=2)
    w_in_b, w_s_b = w_in.astype(BF16), w_s.astype(BF16)
    w_a_b, w_b_b, w_c_b, w_o_b = (w.astype(BF16) for w in (w_a, w_b, w_c, w_o))
    w1_b, w3_b, w2_b = (w.astype(BF16) for w in (w_ff1, w_ff3, w_ff2))

    xc = jnp.concatenate([x.reshape(bsz * s_len, d), ctx.reshape(bsz * l_len, d)], axis=0)
    for l in range(depth):
        n_tiles = geo["n_lat"] if l == depth - 1 else n_all
        ab, z, q, kk, vt, gu, svn, gate = _proj(xc, mods, l, n1, w_in_b, qg, kg, sgn, rope, e, geo)
        attn = _attn(q, kk, vt, geo, n_tiles)
        xc = _merge(xc, mods, l, ab, z, attn, gu, svn, gate, conv_w, w_s_b, bias_tab,
                    w_a_b, w_b_b, w_c_b, w_o_b, geo, n_tiles)
        xc = _ffn(xc, mods, l, n2, w1_b, w3_b, w2_b, geo, n_tiles)
    return xc.reshape(bsz, s_len, d)
```

```python
import functools
import math

import jax
import jax.numpy as jnp
from jax import lax
from jax.experimental import pallas as pl
from jax.experimental.pallas import tpu as pltpu

F32 = jnp.float32
BF16 = jnp.bfloat16

GRID_W = 64
HEAD_DIM = 64
N_Q_HEADS = 8
N_KV_HEADS = 2
GROUP = N_Q_HEADS // N_KV_HEADS
ATTN_WIDTH = N_Q_HEADS * HEAD_DIM
KV_WIDTH = N_KV_HEADS * HEAD_DIM
AXIS_DIM = HEAD_DIM // 2
ROPE_THETA = 10000.0
CONV_K = 3
CHUNK = 128
SG_GROUPS = 4
N_BRANCH = 3
N_MOD = 6
EPS = 1e-6

LANES = 128
SUBLANES = 8
TM = 256
TK = 512
ONES_ROWS = 16
VT_ROWS = HEAD_DIM + ONES_ROWS
Q_SCALE = HEAD_DIM ** -0.5 * math.log2(math.e)
SKEW = 2
MOD_ROWS = 8
VMEM_LIMIT = 56 * 1024 * 1024


def _cparams(n_axes=1):
    return pltpu.CompilerParams(dimension_semantics=("arbitrary",) * n_axes,
                                vmem_limit_bytes=VMEM_LIMIT)


def _dot(a, b):
    return jnp.dot(a, b, preferred_element_type=F32)


def _dot_t(a, b):
    return lax.dot_general(a, b, (((1,), (1,)), ((), ())), preferred_element_type=F32)


def _mod_kernel(cond_ref, w_ref, b_ref, o_ref):
    cnd = cond_ref[...]
    s = (cnd * jax.nn.sigmoid(cnd)).astype(BF16)
    o_ref[...] = _dot(s, w_ref[...].astype(BF16)) + b_ref[...]


def _adaln_all(cond, w_mod, b_mod):
    depth, d, width = w_mod.shape
    tn = width // 4
    return pl.pallas_call(
        _mod_kernel,
        grid=(depth, width // tn),
        in_specs=[
            pl.BlockSpec((MOD_ROWS, d), lambda l, j: (0, 0)),
            pl.BlockSpec((None, d, tn), lambda l, j: (l, 0, j)),
            pl.BlockSpec((None, 1, tn), lambda l, j: (l, 0, j)),
        ],
        out_specs=pl.BlockSpec((None, MOD_ROWS, tn), lambda l, j: (l, 0, j)),
        out_shape=jax.ShapeDtypeStruct((depth, MOD_ROWS, width), F32),
        compiler_params=_cparams(2),
        name="adaln",
    )(cond, w_mod, b_mod.reshape(depth, 1, width))


def _rms(x, eps=EPS):
    return x * lax.rsqrt(jnp.mean(x * x, axis=-1, keepdims=True) + eps)


def _gelu_tanh(x):
    c = math.sqrt(2.0 / math.pi)
    return 0.5 * x * (1.0 + jnp.tanh(c * (x + 0.044715 * (x * x * x))))


def _head_mean_sq(p, e):
    sq = p * p
    hi = sq.astype(BF16)
    lo = (sq - hi.astype(F32)).astype(BF16)
    return _dot(hi, e) + _dot(lo, e)


def _rope_slab(t, cos, sa, sb):
    up = pltpu.roll(t, LANES - 16, axis=1)
    dn = pltpu.roll(t, 16, axis=1)
    return t * cos + up * sa + dn * sb


def _proj_kernel(x_ref, mod_ref, n1_ref, w_ref, qg_ref, kg_ref, sgn_ref, rope_ref, e_ref,
                 ab_ref, z_ref, q_ref, kk_ref, vt_ref, gu_ref, svn_ref, gate_ref, *, d, cw, sgw):
    off_q = 3 * cw
    off_k = off_q + ATTN_WIDTH
    off_v = off_k + KV_WIDTH
    off_u = off_v + KV_WIDTH
    off_g = off_u + 2 * sgw

    x = x_ref[...]
    shift = mod_ref[:, 0:d]
    scale = mod_ref[:, d:2 * d]
    h = (_rms(x) * n1_ref[...]) * (1.0 + scale) + shift
    hb = h.astype(BF16)

    pc = _dot(hb, w_ref[:, 0:off_q])
    ab_ref[...] = pc[:, 0:cw]
    z_ref[...] = pc[:, cw:2 * cw] * pc[:, 2 * cw:3 * cw]

    cos = rope_ref[0]
    sa = rope_ref[1]
    sb = rope_ref[2]
    e = e_ref[...]

    pq = _dot(hb, w_ref[:, off_q:off_k])
    qn = pq * lax.rsqrt(_head_mean_sq(pq, e) + EPS) * qg_ref[...]
    for s in range(ATTN_WIDTH // LANES):
        r = _rope_slab(qn[:, s * LANES:(s + 1) * LANES], cos, sa, sb)
        q_ref[:, s * LANES:(s + 1) * LANES] = (r * Q_SCALE).astype(BF16)

    pk = _dot(hb, w_ref[:, off_k:off_v])
    kn = pk * lax.rsqrt(_head_mean_sq(pk, e[0:KV_WIDTH, 0:KV_WIDTH]) + EPS) * kg_ref[...]
    kr = _rope_slab(kn, cos, sa, sb)
    ksw = pltpu.roll(kr, HEAD_DIM, axis=1)
    lane = lax.broadcasted_iota(jnp.int32, kr.shape, 1)
    lo_half = lane < HEAD_DIM
    zero = jnp.zeros_like(kr)
    kk_ref[:, 0 * LANES:1 * LANES] = jnp.where(lo_half, kr, zero).astype(BF16)
    kk_ref[:, 1 * LANES:2 * LANES] = jnp.where(lo_half, zero, ksw).astype(BF16)
    kk_ref[:, 2 * LANES:3 * LANES] = jnp.where(lo_half, ksw, zero).astype(BF16)
    kk_ref[:, 3 * LANES:4 * LANES] = jnp.where(lo_half, zero, kr).astype(BF16)

    pvt = _dot(hb, w_ref[:, off_v:off_u]).T.astype(BF16)
    ones = jnp.ones((ONES_ROWS, pvt.shape[1]), BF16)
    for hh in range(N_KV_HEADS):
        vt_ref[hh * VT_ROWS:hh * VT_ROWS + HEAD_DIM, :] = pvt[hh * HEAD_DIM:(hh + 1) * HEAD_DIM, :]
        vt_ref[hh * VT_ROWS + HEAD_DIM:(hh + 1) * VT_ROWS, :] = ones

    pu = _dot(hb, w_ref[:, off_u:off_g])
    gu_ref[...] = _gelu_tanh(pu[:, 0:sgw])
    svn_ref[...] = (_rms(_gelu_tanh(pu[:, sgw:2 * sgw])) * sgn_ref[...]).astype(BF16)

    for j in range(N_BRANCH):
        pg = _dot(hb, w_ref[:, off_g + j * d:off_g + (j + 1) * d])
        gate_ref[:, j * d:(j + 1) * d] = jax.nn.sigmoid(pg)


def _proj(xc, mods, l, norm1, w_in, qg, kg, sgn, rope, e, geo):
    ntok, d = xc.shape
    in_w = w_in.shape[-1]
    cw = d // 4
    sgw = d // 4
    n_lat, n_s, bsz = geo["n_lat"], geo["n_s"], geo["B"]
    n_t = ntok // TM

    def mod_row(i):
        return jnp.where(i < n_lat, i // n_s, bsz)

    def rope_blk(i):
        return jnp.where(i < n_lat, i % n_s, n_s)

    row = lambda w: pl.BlockSpec((TM, w), lambda i: (i, 0))
    outs = [(cw, F32), (cw, F32), (ATTN_WIDTH, BF16), (4 * LANES, BF16), None,
            (sgw, F32), (sgw, BF16), (N_BRANCH * d, F32)]
    vt_rows = N_KV_HEADS * VT_ROWS
    out_specs = [pl.BlockSpec((vt_rows, TM), lambda i: (0, i)) if o is None else row(o[0])
                 for o in outs]
    out_shape = [jax.ShapeDtypeStruct((vt_rows, ntok), BF16) if o is None
                 else jax.ShapeDtypeStruct((ntok, o[0]), o[1]) for o in outs]
    return pl.pallas_call(
        functools.partial(_proj_kernel, d=d, cw=cw, sgw=sgw),
        grid=(n_t,),
        in_specs=[
            row(d),
            pl.BlockSpec((None, None, 1, N_MOD * d), lambda i: (l, mod_row(i), 0, 0)),
            pl.BlockSpec((None, 1, d), lambda i: (l, 0, 0)),
            pl.BlockSpec((None, d, in_w), lambda i: (l, 0, 0)),
            pl.BlockSpec((None, 1, ATTN_WIDTH), lambda i: (l, 0, 0)),
            pl.BlockSpec((None, 1, KV_WIDTH), lambda i: (l, 0, 0)),
            pl.BlockSpec((None, 1, sgw), lambda i: (l, 0, 0)),
            pl.BlockSpec((3, TM, LANES), lambda i: (0, rope_blk(i), 0)),
            pl.BlockSpec((ATTN_WIDTH, ATTN_WIDTH), lambda i: (0, 0)),
        ],
        out_specs=out_specs,
        out_shape=out_shape,
        compiler_params=_cparams(1),
        name="proj",
    )(xc, mods, norm1, w_in, qg, kg, sgn, rope, e)


def _attn_kernel(q_ref, qn_ref, kl_ref, vl_ref, kc_ref, vc_ref, kn_ref, o_ref,
                 m_scr, acc_scr, sa_scr, sb_scr, sc_scr, *, n_lat, s_len):
    i = pl.program_id(0)
    n_steps = s_len // TK
    m_scr[...] = jnp.full(m_scr.shape, -jnp.inf, F32)
    acc_scr[...] = jnp.zeros(acc_scr.shape, F32)

    def scores(k_tile, hd, qr=q_ref):
        slab, par, kvh = hd // 2, hd % 2, hd // GROUP
        qs = qr[:, slab * LANES:(slab + 1) * LANES]
        kcol = (2 * kvh + par) * LANES
        return _dot_t(k_tile[:, kcol:kcol + LANES], qs)

    def consume(s, hd, vt_tile):
        kvh = hd // GROUP
        m_prev = m_scr[hd]
        m_new = jnp.maximum(m_prev, jnp.max(s, axis=0, keepdims=True))
        alpha = jnp.exp2(m_prev - m_new)
        p = jnp.exp2(s - m_new).astype(BF16)
        pv = _dot(vt_tile[kvh * VT_ROWS:(kvh + 1) * VT_ROWS, :], p)
        acc_scr[hd] = acc_scr[hd] * alpha + pv
        m_scr[hd] = m_new

    def step(cur_scr, vt_tile, next_k, next_scr, next_q=q_ref):
        if next_k is not None:
            for hd in range(SKEW):
                next_scr[hd] = scores(next_k, hd, next_q)
        for hd in range(N_Q_HEADS):
            if next_k is not None and hd + SKEW < N_Q_HEADS:
                next_scr[hd + SKEW] = scores(next_k, hd + SKEW, next_q)
            consume(cur_scr[hd], hd, vt_tile)

    def k_at(j):
        return kl_ref[pl.ds(pl.multiple_of(j * TK, TK), TK), :]

    def v_at(j):
        return vl_ref[:, pl.ds(pl.multiple_of(j * TK, TK), TK)]

    @pl.when(i == 0)
    def _():
        for hd in range(N_Q_HEADS):
            sa_scr[hd] = scores(k_at(0), hd)

    @pl.when(i < n_lat)
    def _():
        def body(jj, carry):
            j = 2 * jj
            step(sa_scr, v_at(j), k_at(j + 1), sb_scr)
            step(sb_scr, v_at(j + 1), k_at(j + 2), sa_scr)
            return carry
        lax.fori_loop(0, n_steps // 2 - 1, body, 0)
        step(sa_scr, v_at(n_steps - 2), k_at(n_steps - 1), sb_scr)
        step(sb_scr, v_at(n_steps - 1), kc_ref[...], sc_scr)

    @pl.when(i >= n_lat)
    def _():
        for hd in range(N_Q_HEADS):
            sc_scr[hd] = scores(kc_ref[...], hd)

    @pl.when(i < n_lat - 1)
    def _():
        step(sc_scr, vc_ref[...], kn_ref[...], sa_scr, qn_ref)

    @pl.when(i >= n_lat - 1)
    def _():
        step(sc_scr, vc_ref[...], None, None)

    for slab in range(N_Q_HEADS // 2):
        halves = []
        for hd in (2 * slab, 2 * slab + 1):
            a = acc_scr[hd]
            halves.append(a[0:HEAD_DIM, :] / a[HEAD_DIM:HEAD_DIM + 1, :])
        out = jnp.concatenate(halves, axis=0).T
        o_ref[:, slab * LANES:(slab + 1) * LANES] = out.astype(o_ref.dtype)


def _attn(q, kk, vt, geo, n_tiles):
    n_lat, n_s, n_l, bsz = geo["n_lat"], geo["n_s"], geo["n_l"], geo["B"]
    s_len, l_len = n_s * TM, n_l * TM
    ctx_blk0 = (bsz * s_len) // l_len
    vt_rows = N_KV_HEADS * VT_ROWS

    def lat_b(i):
        return jnp.minimum(i // n_s, bsz - 1)

    def ctx_b(i):
        return ctx_blk0 + jnp.where(i < n_lat, i // n_s, (i - n_lat) // n_l)

    def nxt(i):
        return jnp.minimum(i + 1, n_tiles - 1)

    return pl.pallas_call(
        functools.partial(_attn_kernel, n_lat=n_lat, s_len=s_len),
        grid=(n_tiles,),
        in_specs=[
            pl.BlockSpec((TM, ATTN_WIDTH), lambda i: (i, 0)),
            pl.BlockSpec((TM, ATTN_WIDTH), lambda i: (nxt(i), 0)),
            pl.BlockSpec((s_len, 4 * LANES), lambda i: (lat_b(i), 0)),
            pl.BlockSpec((vt_rows, s_len), lambda i: (0, lat_b(i))),
            pl.BlockSpec((l_len, 4 * LANES), lambda i: (ctx_b(i), 0)),
            pl.BlockSpec((vt_rows, l_len), lambda i: (0, ctx_b(i))),
            pl.BlockSpec((TK, 4 * LANES), lambda i: (lat_b(nxt(i)) * (s_len // TK), 0)),
        ],
        out_specs=pl.BlockSpec((TM, ATTN_WIDTH), lambda i: (i, 0)),
        out_shape=jax.ShapeDtypeStruct((n_tiles * TM, ATTN_WIDTH), BF16),
        scratch_shapes=[pltpu.VMEM((N_Q_HEADS, 1, TM), F32),
                        pltpu.VMEM((N_Q_HEADS, VT_ROWS, TM), F32),
                        pltpu.VMEM((N_Q_HEADS, TK, TM), F32),
                        pltpu.VMEM((N_Q_HEADS, TK, TM), F32),
                        pltpu.VMEM((N_Q_HEADS, l_len, TM), F32)],
        compiler_params=_cparams(1),
        name="attn",
    )(q, q, kk, vt, kk, vt, kk)


def _merge_ffn_kernel(x_ref, mod_ref, ab_ref, z_ref, zp_ref, zn_ref, at_ref, gu_ref, svn_ref,
                      gate_ref, cw_ref, ws_ref, bt_ref, wa_ref, wb_ref, wc_ref, wo_ref,
                      n2_ref, w1_ref, w3_ref, w2_ref, o_ref, *, d, n_lat, n_s, n_l, ff_chunk):
    i = pl.program_id(0)
    tm = x_ref.shape[0]
    t_in_seq = jnp.where(i < n_lat, i % n_s, (i - n_lat) % n_l)
    seq_tiles = jnp.where(i < n_lat, n_s, n_l)
    has_prev = (t_in_seq > 0).astype(F32)
    has_next = (t_in_seq < seq_tiles - 1).astype(F32)

    z = z_ref[...]
    row = lax.broadcasted_iota(jnp.int32, z.shape, 0)
    prev_row = zp_ref[SUBLANES - 1:SUBLANES, :] * has_prev
    next_row = zn_ref[0:1, :] * has_next
    zp = jnp.where(row == 0, prev_row, pltpu.roll(z, 1, axis=0))
    zn = jnp.where(row == tm - 1, next_row, pltpu.roll(z, tm - 1, axis=0))
    conv = zp * cw_ref[0:1, :] + z * cw_ref[1:2, :] + zn * cw_ref[2:3, :]
    ya = _dot((ab_ref[...] * conv).astype(BF16), wa_ref[...])

    yb = _dot(at_ref[...], wb_ref[...])

    sgw = svn_ref.shape[1]
    gw = sgw // SG_GROUPS
    lane = lax.broadcasted_iota(jnp.int32, (CHUNK, sgw), 1)
    parts = []
    for c in range(tm // CHUNK):
        vc = svn_ref[c * CHUNK:(c + 1) * CHUNK, :]
        mixed = bt_ref[...]
        for g in range(SG_GROUPS):
            in_g = (lane >= g * gw) & (lane < (g + 1) * gw)
            mixed = mixed + _dot(ws_ref[g], jnp.where(in_g, vc, jnp.zeros_like(vc)))
        parts.append(gu_ref[c * CHUNK:(c + 1) * CHUNK, :] * mixed)
    yc = _dot(jnp.concatenate(parts, axis=0).astype(BF16), wc_ref[...])

    y = gate_ref[:, 0:d] * ya + gate_ref[:, d:2 * d] * yb + gate_ref[:, 2 * d:3 * d] * yc
    out = _dot(y.astype(BF16), wo_ref[...])
    x1 = x_ref[...] + mod_ref[:, 2 * d:3 * d] * out

    shift = mod_ref[:, 3 * d:4 * d]
    scale = mod_ref[:, 4 * d:5 * d]
    hb = ((_rms(x1) * n2_ref[...]) * (1.0 + scale) + shift).astype(BF16)
    d_ff = w1_ref.shape[1]
    acc = None
    for c in range(d_ff // ff_chunk):
        cs = slice(c * ff_chunk, (c + 1) * ff_chunk)
        a = _dot(hb, w1_ref[:, cs])
        b = _dot(hb, w3_ref[:, cs])
        act = ((a * jax.nn.sigmoid(a)) * b).astype(BF16)
        part = _dot(act, w2_ref[cs, :])
        acc = part if acc is None else acc + part
    o_ref[...] = x1 + mod_ref[:, 5 * d:6 * d] * acc


def _merge_ffn(xc, mods, l, ab, z, attn, gu, svn, gate, conv_w, w_s, bias_tab, w_a, w_b, w_c, w_o,
               norm2, w1, w3, w2, geo, n_tiles):
    d = xc.shape[1]
    cw = ab.shape[1]
    sgw = gu.shape[1]
    d_ff = w1.shape[-1]
    ff_chunk = d_ff // 2 if (d_ff // 2) % LANES == 0 else d_ff
    n_lat, n_s, n_l, bsz = geo["n_lat"], geo["n_s"], geo["n_l"], geo["B"]
    rows_per_tile = TM // SUBLANES
    last_blk = z.shape[0] // SUBLANES - 1

    def mod_row(i):
        return jnp.where(i < n_lat, i // n_s, bsz)

    row = lambda w: pl.BlockSpec((TM, w), lambda i: (i, 0))
    whole = lambda a, b: pl.BlockSpec((None, a, b), lambda i: (l, 0, 0),
                                      pipeline_mode=pl.Buffered(1))
    return pl.pallas_call(
        functools.partial(_merge_ffn_kernel, d=d, n_lat=n_lat, n_s=n_s, n_l=n_l,
                          ff_chunk=ff_chunk),
        grid=(n_tiles,),
        in_specs=[
            row(d),
            pl.BlockSpec((None, None, 1, N_MOD * d), lambda i: (l, mod_row(i), 0, 0)),
            row(cw),
            row(cw),
            pl.BlockSpec((SUBLANES, cw), lambda i: (jnp.maximum(i * rows_per_tile - 1, 0), 0)),
            pl.BlockSpec((SUBLANES, cw),
                         lambda i: (jnp.minimum((i + 1) * rows_per_tile, last_blk), 0)),
            row(ATTN_WIDTH),
            row(sgw),
            row(sgw),
            row(N_BRANCH * d),
            whole(CONV_K, cw),
            pl.BlockSpec((None, SG_GROUPS, CHUNK, CHUNK), lambda i: (l, 0, 0, 0),
                         pipeline_mode=pl.Buffered(1)),
            whole(CHUNK, sgw),
            whole(cw, d),
            whole(ATTN_WIDTH, d),
            whole(sgw, d),
            whole(d, d),
            whole(1, d),
            whole(d, d_ff),
            whole(d, d_ff),
            whole(d_ff, d),
        ],
        out_specs=row(d),
        out_shape=jax.ShapeDtypeStruct((n_tiles * TM, d), F32),
        compiler_params=_cparams(1),
        name="merge_ffn",
    )(xc, mods, ab, z, z, z, attn, gu, svn, gate, conv_w, w_s, bias_tab, w_a, w_b, w_c, w_o,
      norm2, w1, w3, w2)


def _rope_table(s_len):
    t = jnp.arange(s_len, dtype=jnp.int32)
    pos = jnp.stack([(t // GRID_W).astype(F32), (t % GRID_W).astype(F32)], axis=1)
    inv_freq = ROPE_THETA ** (-jnp.arange(0, AXIS_DIM, 2, dtype=F32) / AXIS_DIM)
    ang = pos[:, :, None] * inv_freq
    cos, sin = jnp.cos(ang), jnp.sin(ang)
    zero = jnp.zeros_like(sin)
    c64 = jnp.stack([cos, cos], axis=2).reshape(s_len, HEAD_DIM)
    a64 = jnp.stack([-sin, zero], axis=2).reshape(s_len, HEAD_DIM)
    b64 = jnp.stack([zero, sin], axis=2).reshape(s_len, HEAD_DIM)
    tab = jnp.stack([c64, a64, b64], axis=0)
    ident = jnp.stack([jnp.ones((TM, HEAD_DIM), F32), jnp.zeros((TM, HEAD_DIM), F32),
                       jnp.zeros((TM, HEAD_DIM), F32)], axis=0)
    tab = jnp.concatenate([tab, ident], axis=1)
    return jnp.tile(tab, (1, 1, LANES // HEAD_DIM))


def _head_mean_matrix():
    r = jnp.arange(ATTN_WIDTH) // HEAD_DIM
    return jnp.where(r[:, None] == r[None, :], 1.0 / HEAD_DIM, 0.0).astype(BF16)


def kernel(x, c, ctx, c_ctx, w_mod, b_mod, norm1, w_in, q_gain, k_gain, conv_w, sg_norm,
           w_s, b_s, w_a, w_b, w_c, w_o, norm2, w_ff1, w_ff3, w_ff2):
    bsz, s_len, d = x.shape
    l_len = ctx.shape[1]
    depth = w_mod.shape[0]
    sgw = sg_norm.shape[1]
    assert s_len % TM == 0 and l_len % TM == 0 and s_len % (2 * TK) == 0 and s_len % GRID_W == 0
    assert (bsz * s_len) % l_len == 0 and bsz + 1 <= MOD_ROWS
    n_s, n_l = s_len // TM, l_len // TM
    geo = {"B": bsz, "n_s": n_s, "n_l": n_l, "n_lat": bsz * n_s}
    n_all = bsz * (n_s + n_l)

    cond = jnp.concatenate(
        [c, c_ctx[None, :], jnp.zeros((MOD_ROWS - bsz - 1, d), F32)], axis=0)
    mods = _adaln_all(cond, w_mod, b_mod).reshape(depth, MOD_ROWS, 1, N_MOD * d)

    rope = _rope_table(s_len)
    e = _head_mean_matrix()
    qg = jnp.tile(q_gain, (1, N_Q_HEADS)).reshape(depth, 1, ATTN_WIDTH)
    kg = jnp.tile(k_gain, (1, N_KV_HEADS)).reshape(depth, 1, KV_WIDTH)
    sgn = sg_norm.reshape(depth, 1, sgw)
    n1 = norm1.reshape(depth, 1, d)
    n2 = norm2.reshape(depth, 1, d)
    bias_tab = jnp.repeat(jnp.swapaxes(b_s, 1, 2), sgw // SG_GROUPS, axis=2)
    w_in_b, w_s_b = w_in.astype(BF16), w_s.astype(BF16)
    w_a_b, w_b_b, w_c_b, w_o_b = (w.astype(BF16) for w in (w_a, w_b, w_c, w_o))
    w1_b, w3_b, w2_b = (w.astype(BF16) for w in (w_ff1, w_ff3, w_ff2))

    xc = jnp.concatenate([x.reshape(bsz * s_len, d), ctx.reshape(bsz * l_len, d)], axis=0)
    for l in range(depth):
        n_tiles = geo["n_lat"] if l == depth - 1 else n_all
        ab, z, q, kk, vt, gu, svn, gate = _proj(xc, mods, l, n1, w_in_b, qg, kg, sgn, rope, e, geo)
        attn = _attn(q, kk, vt, geo, n_tiles)
        xc = _merge_ffn(xc, mods, l, ab, z, attn, gu, svn, gate, conv_w, w_s_b, bias_tab,
                        w_a_b, w_b_b, w_c_b, w_o_b, n2, w1_b, w3_b, w2_b, geo, n_tiles)
    return xc.reshape(bsz, s_len, d)
```

```python
import functools
import math

import jax
import jax.numpy as jnp
from jax import lax
from jax.experimental import pallas as pl
from jax.experimental.pallas import tpu as pltpu

F32 = jnp.float32
BF16 = jnp.bfloat16

GRID_W = 64
HEAD_DIM = 64
N_Q_HEADS = 8
N_KV_HEADS = 2
GROUP = N_Q_HEADS // N_KV_HEADS
ATTN_WIDTH = N_Q_HEADS * HEAD_DIM
KV_WIDTH = N_KV_HEADS * HEAD_DIM
AXIS_DIM = HEAD_DIM // 2
ROPE_THETA = 10000.0
CONV_K = 3
CHUNK = 128
SG_GROUPS = 4
N_BRANCH = 3
N_MOD = 6
EPS = 1e-6

LANES = 128
SUBLANES = 8
TM = 256
TK = 512
ONES_ROWS = 16
VT_ROWS = HEAD_DIM + ONES_ROWS
Q_SCALE = HEAD_DIM ** -0.5 * math.log2(math.e)
SKEW = 2
UNROLL = 4
MOD_ROWS = 8
VMEM_LIMIT = 56 * 1024 * 1024


def _cparams(n_axes=1):
    return pltpu.CompilerParams(dimension_semantics=("arbitrary",) * n_axes,
                                vmem_limit_bytes=VMEM_LIMIT)


def _dot(a, b):
    return jnp.dot(a, b, preferred_element_type=F32)


def _dot_t(a, b):
    return lax.dot_general(a, b, (((1,), (1,)), ((), ())), preferred_element_type=F32)


def _mod_kernel(cond_ref, w_ref, b_ref, o_ref):
    cnd = cond_ref[...]
    s = (cnd * jax.nn.sigmoid(cnd)).astype(BF16)
    o_ref[...] = _dot(s, w_ref[...].astype(BF16)) + b_ref[...]


def _adaln_all(cond, w_mod, b_mod):
    depth, d, width = w_mod.shape
    tn = width // 4
    return pl.pallas_call(
        _mod_kernel,
        grid=(depth, width // tn),
        in_specs=[
            pl.BlockSpec((MOD_ROWS, d), lambda l, j: (0, 0)),
            pl.BlockSpec((None, d, tn), lambda l, j: (l, 0, j)),
            pl.BlockSpec((None, 1, tn), lambda l, j: (l, 0, j)),
        ],
        out_specs=pl.BlockSpec((None, MOD_ROWS, tn), lambda l, j: (l, 0, j)),
        out_shape=jax.ShapeDtypeStruct((depth, MOD_ROWS, width), F32),
        compiler_params=_cparams(2),
        name="adaln",
    )(cond, w_mod, b_mod.reshape(depth, 1, width))


def _rms(x, eps=EPS):
    return x * lax.rsqrt(jnp.mean(x * x, axis=-1, keepdims=True) + eps)


def _gelu_tanh(x):
    c = math.sqrt(2.0 / math.pi)
    return 0.5 * x * (1.0 + jnp.tanh(c * (x + 0.044715 * (x * x * x))))


def _head_mean_sq(p, e):
    sq = p * p
    hi = sq.astype(BF16)
    lo = (sq - hi.astype(F32)).astype(BF16)
    return _dot(hi, e) + _dot(lo, e)


def _rope_slab(t, cos, sa, sb):
    up = pltpu.roll(t, LANES - 16, axis=1)
    dn = pltpu.roll(t, 16, axis=1)
    return t * cos + up * sa + dn * sb


def _proj_kernel(x_ref, mod_ref, n1_ref, w_ref, qg_ref, kg_ref, sgn_ref, rope_ref, e_ref,
                 ab_ref, z_ref, q_ref, kk_ref, vt_ref, gu_ref, svn_ref, gate_ref, *, d, cw, sgw):
    off_q = 3 * cw
    off_k = off_q + ATTN_WIDTH
    off_v = off_k + KV_WIDTH
    off_u = off_v + KV_WIDTH
    off_g = off_u + 2 * sgw

    x = x_ref[...]
    shift = mod_ref[:, 0:d]
    scale = mod_ref[:, d:2 * d]
    h = (_rms(x) * n1_ref[...]) * (1.0 + scale) + shift
    hb = h.astype(BF16)

    pc = _dot(hb, w_ref[:, 0:off_q])
    ab_ref[...] = pc[:, 0:cw]
    z_ref[...] = pc[:, cw:2 * cw] * pc[:, 2 * cw:3 * cw]

    cos = rope_ref[0]
    sa = rope_ref[1]
    sb = rope_ref[2]
    e = e_ref[...]

    pq = _dot(hb, w_ref[:, off_q:off_k])
    qn = pq * lax.rsqrt(_head_mean_sq(pq, e) + EPS) * qg_ref[...]
    for s in range(ATTN_WIDTH // LANES):
        r = _rope_slab(qn[:, s * LANES:(s + 1) * LANES], cos, sa, sb)
        q_ref[:, s * LANES:(s + 1) * LANES] = (r * Q_SCALE).astype(BF16)

    pk = _dot(hb, w_ref[:, off_k:off_v])
    kn = pk * lax.rsqrt(_head_mean_sq(pk, e[0:KV_WIDTH, 0:KV_WIDTH]) + EPS) * kg_ref[...]
    kr = _rope_slab(kn, cos, sa, sb)
    ksw = pltpu.roll(kr, HEAD_DIM, axis=1)
    lane = lax.broadcasted_iota(jnp.int32, kr.shape, 1)
    lo_half = lane < HEAD_DIM
    zero = jnp.zeros_like(kr)
    kk_ref[:, 0 * LANES:1 * LANES] = jnp.where(lo_half, kr, zero).astype(BF16)
    kk_ref[:, 1 * LANES:2 * LANES] = jnp.where(lo_half, zero, ksw).astype(BF16)
    kk_ref[:, 2 * LANES:3 * LANES] = jnp.where(lo_half, ksw, zero).astype(BF16)
    kk_ref[:, 3 * LANES:4 * LANES] = jnp.where(lo_half, zero, kr).astype(BF16)

    pvt = _dot(hb, w_ref[:, off_v:off_u]).T.astype(BF16)
    ones = jnp.ones((ONES_ROWS, pvt.shape[1]), BF16)
    for hh in range(N_KV_HEADS):
        vt_ref[hh * VT_ROWS:hh * VT_ROWS + HEAD_DIM, :] = pvt[hh * HEAD_DIM:(hh + 1) * HEAD_DIM, :]
        vt_ref[hh * VT_ROWS + HEAD_DIM:(hh + 1) * VT_ROWS, :] = ones

    pu = _dot(hb, w_ref[:, off_u:off_g])
    gu_ref[...] = _gelu_tanh(pu[:, 0:sgw])
    svn_ref[...] = (_rms(_gelu_tanh(pu[:, sgw:2 * sgw])) * sgn_ref[...]).astype(BF16)

    for j in range(N_BRANCH):
        pg = _dot(hb, w_ref[:, off_g + j * d:off_g + (j + 1) * d])
        gate_ref[:, j * d:(j + 1) * d] = jax.nn.sigmoid(pg)


def _proj(xc, mods, l, norm1, w_in, qg, kg, sgn, rope, e, geo):
    ntok, d = xc.shape
    in_w = w_in.shape[-1]
    cw = d // 4
    sgw = d // 4
    n_lat, n_s, bsz = geo["n_lat"], geo["n_s"], geo["B"]
    n_t = ntok // TM

    def mod_row(i):
        return jnp.where(i < n_lat, i // n_s, bsz)

    def rope_blk(i):
        return jnp.where(i < n_lat, i % n_s, n_s)

    row = lambda w: pl.BlockSpec((TM, w), lambda i: (i, 0))
    outs = [(cw, F32), (cw, F32), (ATTN_WIDTH, BF16), (4 * LANES, BF16), None,
            (sgw, F32), (sgw, BF16), (N_BRANCH * d, F32)]
    vt_rows = N_KV_HEADS * VT_ROWS
    out_specs = [pl.BlockSpec((vt_rows, TM), lambda i: (0, i)) if o is None else row(o[0])
                 for o in outs]
    out_shape = [jax.ShapeDtypeStruct((vt_rows, ntok), BF16) if o is None
                 else jax.ShapeDtypeStruct((ntok, o[0]), o[1]) for o in outs]
    return pl.pallas_call(
        functools.partial(_proj_kernel, d=d, cw=cw, sgw=sgw),
        grid=(n_t,),
        in_specs=[
            row(d),
            pl.BlockSpec((None, None, 1, N_MOD * d), lambda i: (l, mod_row(i), 0, 0)),
            pl.BlockSpec((None, 1, d), lambda i: (l, 0, 0)),
            pl.BlockSpec((None, d, in_w), lambda i: (l, 0, 0)),
            pl.BlockSpec((None, 1, ATTN_WIDTH), lambda i: (l, 0, 0)),
            pl.BlockSpec((None, 1, KV_WIDTH), lambda i: (l, 0, 0)),
            pl.BlockSpec((None, 1, sgw), lambda i: (l, 0, 0)),
            pl.BlockSpec((3, TM, LANES), lambda i: (0, rope_blk(i), 0)),
            pl.BlockSpec((ATTN_WIDTH, ATTN_WIDTH), lambda i: (0, 0)),
        ],
        out_specs=out_specs,
        out_shape=out_shape,
        compiler_params=_cparams(1),
        name="proj",
    )(xc, mods, norm1, w_in, qg, kg, sgn, rope, e)


def _attn_kernel(q_ref, qn_ref, kl_ref, vl_ref, kc_ref, vc_ref, kn_ref, o_ref,
                 m_scr, acc_scr, sa_scr, sb_scr, sc_scr, ma_scr, mb_scr, mc_scr, *, n_lat, s_len):
    i = pl.program_id(0)
    n_steps = s_len // TK
    m_scr[...] = jnp.full(m_scr.shape, -jnp.inf, F32)
    acc_scr[...] = jnp.zeros(acc_scr.shape, F32)

    bufs = {"a": (sa_scr, ma_scr), "b": (sb_scr, mb_scr), "c": (sc_scr, mc_scr)}

    def produce(buf, k_tile, hd, qr=q_ref):
        slab, par, kvh = hd // 2, hd % 2, hd // GROUP
        qs = qr[:, slab * LANES:(slab + 1) * LANES]
        kcol = (2 * kvh + par) * LANES
        s = _dot_t(k_tile[:, kcol:kcol + LANES], qs)
        s_scr, mx_scr = bufs[buf]
        s_scr[hd] = s
        mx_scr[hd] = jnp.max(s, axis=0, keepdims=True)

    def consume(buf, hd, vt_tile):
        s_scr, mx_scr = bufs[buf]
        kvh = hd // GROUP
        m_prev = m_scr[hd]
        m_new = jnp.maximum(m_prev, mx_scr[hd])
        alpha = jnp.exp2(m_prev - m_new)
        p = jnp.exp2(s_scr[hd] - m_new).astype(BF16)
        pv = _dot(vt_tile[kvh * VT_ROWS:(kvh + 1) * VT_ROWS, :], p)
        acc_scr[hd] = acc_scr[hd] * alpha + pv
        m_scr[hd] = m_new

    def step(cur, vt_tile, next_k, nxt, next_q=q_ref):
        if next_k is not None:
            for hd in range(SKEW):
                produce(nxt, next_k, hd, next_q)
        for hd in range(N_Q_HEADS):
            if next_k is not None and hd + SKEW < N_Q_HEADS:
                produce(nxt, next_k, hd + SKEW, next_q)
            consume(cur, hd, vt_tile)

    def k_at(j):
        return kl_ref[pl.ds(pl.multiple_of(j * TK, TK), TK), :]

    def v_at(j):
        return vl_ref[:, pl.ds(pl.multiple_of(j * TK, TK), TK)]

    @pl.when(i == 0)
    def _():
        for hd in range(N_Q_HEADS):
            produce("a", k_at(0), hd)

    @pl.when(i < n_lat)
    def _():
        def body(jj, carry):
            for u in range(UNROLL):
                j = UNROLL * jj + u
                step("ab"[u % 2], v_at(j), k_at(j + 1), "ab"[(u + 1) % 2])
            return carry
        lax.fori_loop(0, n_steps // UNROLL - 1, body, 0)
        for u in range(UNROLL - 1):
            j = n_steps - UNROLL + u
            step("ab"[u % 2], v_at(j), k_at(j + 1), "ab"[(u + 1) % 2])
        step("b", v_at(n_steps - 1), kc_ref[...], "c")

    @pl.when(i >= n_lat)
    def _():
        for hd in range(N_Q_HEADS):
            produce("c", kc_ref[...], hd)

    @pl.when(i < n_lat - 1)
    def _():
        step("c", vc_ref[...], kn_ref[...], "a", qn_ref)

    @pl.when(i >= n_lat - 1)
    def _():
        step("c", vc_ref[...], None, None)

    for slab in range(N_Q_HEADS // 2):
        halves = []
        for hd in (2 * slab, 2 * slab + 1):
            a = acc_scr[hd]
            halves.append(a[0:HEAD_DIM, :] / a[HEAD_DIM:HEAD_DIM + 1, :])
        out = jnp.concatenate(halves, axis=0).T
        o_ref[:, slab * LANES:(slab + 1) * LANES] = out.astype(o_ref.dtype)


def _attn(q, kk, vt, geo, n_tiles):
    n_lat, n_s, n_l, bsz = geo["n_lat"], geo["n_s"], geo["n_l"], geo["B"]
    s_len, l_len = n_s * TM, n_l * TM
    ctx_blk0 = (bsz * s_len) // l_len
    vt_rows = N_KV_HEADS * VT_ROWS

    def lat_b(i):
        return jnp.minimum(i // n_s, bsz - 1)

    def ctx_b(i):
        return ctx_blk0 + jnp.where(i < n_lat, i // n_s, (i - n_lat) // n_l)

    def nxt(i):
        return jnp.minimum(i + 1, n_tiles - 1)

    return pl.pallas_call(
        functools.partial(_attn_kernel, n_lat=n_lat, s_len=s_len),
        grid=(n_tiles,),
        in_specs=[
            pl.BlockSpec((TM, ATTN_WIDTH), lambda i: (i, 0)),
            pl.BlockSpec((TM, ATTN_WIDTH), lambda i: (nxt(i), 0)),
            pl.BlockSpec((s_len, 4 * LANES), lambda i: (lat_b(i), 0)),
            pl.BlockSpec((vt_rows, s_len), lambda i: (0, lat_b(i))),
            pl.BlockSpec((l_len, 4 * LANES), lambda i: (ctx_b(i), 0)),
            pl.BlockSpec((vt_rows, l_len), lambda i: (0, ctx_b(i))),
            pl.BlockSpec((TK, 4 * LANES), lambda i: (lat_b(nxt(i)) * (s_len // TK), 0)),
        ],
        out_specs=pl.BlockSpec((TM, ATTN_WIDTH), lambda i: (i, 0)),
        out_shape=jax.ShapeDtypeStruct((n_tiles * TM, ATTN_WIDTH), BF16),
        scratch_shapes=[pltpu.VMEM((N_Q_HEADS, 1, TM), F32),
                        pltpu.VMEM((N_Q_HEADS, VT_ROWS, TM), F32),
                        pltpu.VMEM((N_Q_HEADS, TK, TM), F32),
                        pltpu.VMEM((N_Q_HEADS, TK, TM), F32),
                        pltpu.VMEM((N_Q_HEADS, l_len, TM), F32),
                        pltpu.VMEM((N_Q_HEADS, 1, TM), F32),
                        pltpu.VMEM((N_Q_HEADS, 1, TM), F32),
                        pltpu.VMEM((N_Q_HEADS, 1, TM), F32)],
        compiler_params=_cparams(1),
        name="attn",
    )(q, q, kk, vt, kk, vt, kk)


def _merge_ffn_kernel(x_ref, mod_ref, ab_ref, z_ref, zp_ref, zn_ref, at_ref, gu_ref, svn_ref,
                      gate_ref, cw_ref, ws_ref, bt_ref, wa_ref, wb_ref, wc_ref, wo_ref,
                      n2_ref, w1_ref, w3_ref, w2_ref, o_ref, *, d, n_lat, n_s, n_l, ff_chunk):
    i = pl.program_id(0)
    tm = x_ref.shape[0]
    t_in_seq = jnp.where(i < n_lat, i % n_s, (i - n_lat) % n_l)
    seq_tiles = jnp.where(i < n_lat, n_s, n_l)
    has_prev = (t_in_seq > 0).astype(F32)
    has_next = (t_in_seq < seq_tiles - 1).astype(F32)

    z = z_ref[...]
    row = lax.broadcasted_iota(jnp.int32, z.shape, 0)
    prev_row = zp_ref[SUBLANES - 1:SUBLANES, :] * has_prev
    next_row = zn_ref[0:1, :] * has_next
    zp = jnp.where(row == 0, prev_row, pltpu.roll(z, 1, axis=0))
    zn = jnp.where(row == tm - 1, next_row, pltpu.roll(z, tm - 1, axis=0))
    conv = zp * cw_ref[0:1, :] + z * cw_ref[1:2, :] + zn * cw_ref[2:3, :]
    ya = _dot((ab_ref[...] * conv).astype(BF16), wa_ref[...])

    yb = _dot(at_ref[...], wb_ref[...])

    sgw = svn_ref.shape[1]
    gw = sgw // SG_GROUPS
    lane = lax.broadcasted_iota(jnp.int32, (CHUNK, sgw), 1)
    parts = []
    for c in range(tm // CHUNK):
        vc = svn_ref[c * CHUNK:(c + 1) * CHUNK, :]
        mixed = bt_ref[...]
        for g in range(SG_GROUPS):
            in_g = (lane >= g * gw) & (lane < (g + 1) * gw)
            mixed = mixed + _dot(ws_ref[g], jnp.where(in_g, vc, jnp.zeros_like(vc)))
        parts.append(gu_ref[c * CHUNK:(c + 1) * CHUNK, :] * mixed)
    yc = _dot(jnp.concatenate(parts, axis=0).astype(BF16), wc_ref[...])

    y = gate_ref[:, 0:d] * ya + gate_ref[:, d:2 * d] * yb + gate_ref[:, 2 * d:3 * d] * yc
    out = _dot(y.astype(BF16), wo_ref[...])
    x1 = x_ref[...] + mod_ref[:, 2 * d:3 * d] * out

    shift = mod_ref[:, 3 * d:4 * d]
    scale = mod_ref[:, 4 * d:5 * d]
    hb = ((_rms(x1) * n2_ref[...]) * (1.0 + scale) + shift).astype(BF16)
    d_ff = w1_ref.shape[1]
    acc = None
    for c in range(d_ff // ff_chunk):
        cs = slice(c * ff_chunk, (c + 1) * ff_chunk)
        a = _dot(hb, w1_ref[:, cs])
        b = _dot(hb, w3_ref[:, cs])
        act = ((a * jax.nn.sigmoid(a)) * b).astype(BF16)
        part = _dot(act, w2_ref[cs, :])
        acc = part if acc is None else acc + part
    o_ref[...] = x1 + mod_ref[:, 5 * d:6 * d] * acc


def _merge_ffn(xc, mods, l, ab, z, attn, gu, svn, gate, conv_w, w_s, bias_tab, w_a, w_b, w_c, w_o,
               norm2, w1, w3, w2, geo, n_tiles):
    d = xc.shape[1]
    cw = ab.shape[1]
    sgw = gu.shape[1]
    d_ff = w1.shape[-1]
    ff_chunk = d_ff // 2 if (d_ff // 2) % LANES == 0 else d_ff
    n_lat, n_s, n_l, bsz = geo["n_lat"], geo["n_s"], geo["n_l"], geo["B"]
    rows_per_tile = TM // SUBLANES
    last_blk = z.shape[0] // SUBLANES - 1

    def mod_row(i):
        return jnp.where(i < n_lat, i // n_s, bsz)

    row = lambda w: pl.BlockSpec((TM, w), lambda i: (i, 0))
    whole = lambda a, b: pl.BlockSpec((None, a, b), lambda i: (l, 0, 0),
                                      pipeline_mode=pl.Buffered(1))
    return pl.pallas_call(
        functools.partial(_merge_ffn_kernel, d=d, n_lat=n_lat, n_s=n_s, n_l=n_l,
                          ff_chunk=ff_chunk),
        grid=(n_tiles,),
        in_specs=[
            row(d),
            pl.BlockSpec((None, None, 1, N_MOD * d), lambda i: (l, mod_row(i), 0, 0)),
            row(cw),
            row(cw),
            pl.BlockSpec((SUBLANES, cw), lambda i: (jnp.maximum(i * rows_per_tile - 1, 0), 0)),
            pl.BlockSpec((SUBLANES, cw),
                         lambda i: (jnp.minimum((i + 1) * rows_per_tile, last_blk), 0)),
            row(ATTN_WIDTH),
            row(sgw),
            row(sgw),
            row(N_BRANCH * d),
            whole(CONV_K, cw),
            pl.BlockSpec((None, SG_GROUPS, CHUNK, CHUNK), lambda i: (l, 0, 0, 0),
                         pipeline_mode=pl.Buffered(1)),
            whole(CHUNK, sgw),
            whole(cw, d),
            whole(ATTN_WIDTH, d),
            whole(sgw, d),
            whole(d, d),
            whole(1, d),
            whole(d, d_ff),
            whole(d, d_ff),
            whole(d_ff, d),
        ],
        out_specs=row(d),
        out_shape=jax.ShapeDtypeStruct((n_tiles * TM, d), F32),
        compiler_params=_cparams(1),
        name="merge_ffn",
    )(xc, mods, ab, z, z, z, attn, gu, svn, gate, conv_w, w_s, bias_tab, w_a, w_b, w_c, w_o,
      norm2, w1, w3, w2)


def _rope_table(s_len):
    t = jnp.arange(s_len, dtype=jnp.int32)
    pos = jnp.stack([(t // GRID_W).astype(F32), (t % GRID_W).astype(F32)], axis=1)
    inv_freq = ROPE_THETA ** (-jnp.arange(0, AXIS_DIM, 2, dtype=F32) / AXIS_DIM)
    ang = pos[:, :, None] * inv_freq
    cos, sin = jnp.cos(ang), jnp.sin(ang)
    zero = jnp.zeros_like(sin)
    c64 = jnp.stack([cos, cos], axis=2).reshape(s_len, HEAD_DIM)
    a64 = jnp.stack([-sin, zero], axis=2).reshape(s_len, HEAD_DIM)
    b64 = jnp.stack([zero, sin], axis=2).reshape(s_len, HEAD_DIM)
    tab = jnp.stack([c64, a64, b64], axis=0)
    ident = jnp.stack([jnp.ones((TM, HEAD_DIM), F32), jnp.zeros((TM, HEAD_DIM), F32),
                       jnp.zeros((TM, HEAD_DIM), F32)], axis=0)
    tab = jnp.concatenate([tab, ident], axis=1)
    return jnp.tile(tab, (1, 1, LANES // HEAD_DIM))


def _head_mean_matrix():
    r = jnp.arange(ATTN_WIDTH) // HEAD_DIM
    return jnp.where(r[:, None] == r[None, :], 1.0 / HEAD_DIM, 0.0).astype(BF16)


def kernel(x, c, ctx, c_ctx, w_mod, b_mod, norm1, w_in, q_gain, k_gain, conv_w, sg_norm,
           w_s, b_s, w_a, w_b, w_c, w_o, norm2, w_ff1, w_ff3, w_ff2):
    bsz, s_len, d = x.shape
    l_len = ctx.shape[1]
    depth = w_mod.shape[0]
    sgw = sg_norm.shape[1]
    assert s_len % TM == 0 and l_len % TM == 0 and s_len % (UNROLL * TK) == 0 and s_len % GRID_W == 0
    assert (bsz * s_len) % l_len == 0 and bsz + 1 <= MOD_ROWS
    n_s, n_l = s_len // TM, l_len // TM
    geo = {"B": bsz, "n_s": n_s, "n_l": n_l, "n_lat": bsz * n_s}
    n_all = bsz * (n_s + n_l)

    cond = jnp.concatenate(
        [c, c_ctx[None, :], jnp.zeros((MOD_ROWS - bsz - 1, d), F32)], axis=0)
    mods = _adaln_all(cond, w_mod, b_mod).reshape(depth, MOD_ROWS, 1, N_MOD * d)

    rope = _rope_table(s_len)
    e = _head_mean_matrix()
    qg = jnp.tile(q_gain, (1, N_Q_HEADS)).reshape(depth, 1, ATTN_WIDTH)
    kg = jnp.tile(k_gain, (1, N_KV_HEADS)).reshape(depth, 1, KV_WIDTH)
    sgn = sg_norm.reshape(depth, 1, sgw)
    n1 = norm1.reshape(depth, 1, d)
    n2 = norm2.reshape(depth, 1, d)
    bias_tab = jnp.repeat(jnp.swapaxes(b_s, 1, 2), sgw // SG_GROUPS, axis=2)
    w_in_b, w_s_b = w_in.astype(BF16), w_s.astype(BF16)
    w_a_b, w_b_b, w_c_b, w_o_b = (w.astype(BF16) for w in (w_a, w_b, w_c, w_o))
    w1_b, w3_b, w2_b = (w.astype(BF16) for w in (w_ff1, w_ff3, w_ff2))

    xc = jnp.concatenate([x.reshape(bsz * s_len, d), ctx.reshape(bsz * l_len, d)], axis=0)
    for l in range(depth):
        n_tiles = geo["n_lat"] if l == depth - 1 else n_all
        ab, z, q, kk, vt, gu, svn, gate = _proj(xc, mods, l, n1, w_in_b, qg, kg, sgn, rope, e, geo)
        attn = _attn(q, kk, vt, geo, n_tiles)
        xc = _merge_ffn(xc, mods, l, ab, z, attn, gu, svn, gate, conv_w, w_s_b, bias_tab,
                        w_a_b, w_b_b, w_c_b, w_o_b, n2, w1_b, w3_b, w2_b, geo, n_tiles)
    return xc.reshape(bsz, s_len, d)
```

```python
import functools
import math

import jax
import jax.numpy as jnp
from jax import lax
from jax.experimental import pallas as pl
from jax.experimental.pallas import tpu as pltpu

F32 = jnp.float32
BF16 = jnp.bfloat16

GRID_W = 64
HEAD_DIM = 64
N_Q_HEADS = 8
N_KV_HEADS = 2
GROUP = N_Q_HEADS // N_KV_HEADS
ATTN_WIDTH = N_Q_HEADS * HEAD_DIM
KV_WIDTH = N_KV_HEADS * HEAD_DIM
AXIS_DIM = HEAD_DIM // 2
ROPE_THETA = 10000.0
CONV_K = 3
CHUNK = 128
SG_GROUPS = 4
N_BRANCH = 3
N_MOD = 6
EPS = 1e-6

LANES = 128
MXU_DIM = 256
SUBLANES = 8
TM = 256
TP = 512
TK = 512
ONES_ROWS = 16
VT_ROWS = HEAD_DIM + ONES_ROWS
Q_SCALE = HEAD_DIM ** -0.5 * math.log2(math.e)
SKEW = 2
UNROLL = 4
MOD_ROWS = 8
VMEM_LIMIT = 56 * 1024 * 1024


def _cparams(n_axes=1):
    return pltpu.CompilerParams(dimension_semantics=("arbitrary",) * n_axes,
                                vmem_limit_bytes=VMEM_LIMIT)


def _dot(a, b):
    return jnp.dot(a, b, preferred_element_type=F32)


def _dot_t(a, b):
    return lax.dot_general(a, b, (((1,), (1,)), ((), ())), preferred_element_type=F32)


def _mod_kernel(cond_ref, w_ref, b_ref, o_ref):
    cnd = cond_ref[...]
    s = (cnd * jax.nn.sigmoid(cnd)).astype(BF16)
    o_ref[...] = _dot(s, w_ref[...].astype(BF16)) + b_ref[...]


def _adaln_all(cond, w_mod, b_mod):
    depth, d, width = w_mod.shape
    tn = width // 4
    return pl.pallas_call(
        _mod_kernel,
        grid=(depth, width // tn),
        in_specs=[
            pl.BlockSpec((MOD_ROWS, d), lambda l, j: (0, 0)),
            pl.BlockSpec((None, d, tn), lambda l, j: (l, 0, j)),
            pl.BlockSpec((None, 1, tn), lambda l, j: (l, 0, j)),
        ],
        out_specs=pl.BlockSpec((None, MOD_ROWS, tn), lambda l, j: (l, 0, j)),
        out_shape=jax.ShapeDtypeStruct((depth, MOD_ROWS, width), F32),
        compiler_params=_cparams(2),
        name="adaln",
    )(cond, w_mod, b_mod.reshape(depth, 1, width))


def _rms(x, eps=EPS):
    return x * lax.rsqrt(jnp.mean(x * x, axis=-1, keepdims=True) + eps)


def _gelu_tanh(x):
    c = math.sqrt(2.0 / math.pi)
    return 0.5 * x * (1.0 + jnp.tanh(c * (x + 0.044715 * (x * x * x))))


def _head_mean_sq(p, e):
    sq = p * p
    hi = sq.astype(BF16)
    lo = (sq - hi.astype(F32)).astype(BF16)
    blk = min(p.shape[1], e.shape[0])
    eb = e[0:blk, 0:blk]
    cols = [_dot(hi[:, c:c + blk], eb) + _dot(lo[:, c:c + blk], eb)
            for c in range(0, p.shape[1], blk)]
    return cols[0] if len(cols) == 1 else jnp.concatenate(cols, axis=1)


def _rope_slab(t, cos, sa, sb):
    up = pltpu.roll(t, LANES - 16, axis=1)
    dn = pltpu.roll(t, 16, axis=1)
    return t * cos + up * sa + dn * sb


def _proj_kernel(x_ref, mod_ref, n1_ref, w_ref, qg_ref, kg_ref, sgn_ref, rope_ref, e_ref,
                 ab_ref, z_ref, q_ref, kk_ref, vt_ref, gu_ref, svn_ref, gate_ref, *, d, cw, sgw):
    off_q = 3 * cw
    off_k = off_q + ATTN_WIDTH
    off_v = off_k + KV_WIDTH
    off_u = off_v + KV_WIDTH
    off_g = off_u + 2 * sgw

    x = x_ref[...]
    shift = mod_ref[:, 0:d]
    scale = mod_ref[:, d:2 * d]
    h = (_rms(x) * n1_ref[...]) * (1.0 + scale) + shift
    hb = h.astype(BF16)

    cos = rope_ref[0]
    sa = rope_ref[1]
    sb = rope_ref[2]
    e = e_ref[...]

    pq = _dot(hb, w_ref[:, off_q:off_k])
    qn = pq * lax.rsqrt(_head_mean_sq(pq, e) + EPS) * qg_ref[...]
    for s in range(ATTN_WIDTH // LANES):
        r = _rope_slab(qn[:, s * LANES:(s + 1) * LANES], cos, sa, sb)
        q_ref[:, s * LANES:(s + 1) * LANES] = (r * Q_SCALE).astype(BF16)

    pk = _dot(hb, w_ref[:, off_k:off_v])
    kn = pk * lax.rsqrt(_head_mean_sq(pk, e) + EPS) * kg_ref[...]
    kr = _rope_slab(kn, cos, sa, sb)
    ksw = pltpu.roll(kr, HEAD_DIM, axis=1)
    lane = lax.broadcasted_iota(jnp.int32, kr.shape, 1)
    lo_half = lane < HEAD_DIM
    zero = jnp.zeros_like(kr)
    kk_ref[:, 0 * LANES:1 * LANES] = jnp.where(lo_half, kr, zero).astype(BF16)
    kk_ref[:, 1 * LANES:2 * LANES] = jnp.where(lo_half, zero, ksw).astype(BF16)
    kk_ref[:, 2 * LANES:3 * LANES] = jnp.where(lo_half, ksw, zero).astype(BF16)
    kk_ref[:, 3 * LANES:4 * LANES] = jnp.where(lo_half, zero, kr).astype(BF16)

    pvt = _dot(hb, w_ref[:, off_v:off_u]).T.astype(BF16)
    ones = jnp.ones((ONES_ROWS, pvt.shape[1]), BF16)
    for hh in range(N_KV_HEADS):
        vt_ref[hh * VT_ROWS:hh * VT_ROWS + HEAD_DIM, :] = pvt[hh * HEAD_DIM:(hh + 1) * HEAD_DIM, :]
        vt_ref[hh * VT_ROWS + HEAD_DIM:(hh + 1) * VT_ROWS, :] = ones

    pu = _dot(hb, w_ref[:, off_u:off_g])
    gu_ref[...] = _gelu_tanh(pu[:, 0:sgw])
    svn_ref[...] = (_rms(_gelu_tanh(pu[:, sgw:2 * sgw])) * sgn_ref[...]).astype(BF16)

    for j in range(N_BRANCH):
        pg = _dot(hb, w_ref[:, off_g + j * d:off_g + (j + 1) * d])
        gate_ref[:, j * d:(j + 1) * d] = jax.nn.sigmoid(pg)

    pc = _dot(hb, w_ref[:, 0:off_q])
    ab_ref[...] = pc[:, 0:cw]
    z_ref[...] = pc[:, cw:2 * cw] * pc[:, 2 * cw:3 * cw]


def _proj(xc, mods, l, norm1, w_in, qg, kg, sgn, rope, e, geo):
    ntok, d = xc.shape
    in_w = w_in.shape[-1]
    cw = d // 4
    sgw = d // 4
    bsz = geo["B"]
    n_s = geo["n_s"] * TM // TP
    n_lat = bsz * n_s
    n_t = ntok // TP

    def mod_row(i):
        return jnp.where(i < n_lat, i // n_s, bsz)

    def rope_blk(i):
        return jnp.where(i < n_lat, i % n_s, n_s)

    row = lambda w: pl.BlockSpec((TP, w), lambda i: (i, 0))
    const = lambda *blk: pl.BlockSpec((None,) + blk, lambda i: (l,) + (0,) * len(blk),
                                      pipeline_mode=pl.Buffered(1))
    outs = [(cw, F32), (cw, F32), (ATTN_WIDTH, BF16), (4 * LANES, BF16), None,
            (sgw, F32), (sgw, BF16), (N_BRANCH * d, F32)]
    vt_rows = N_KV_HEADS * VT_ROWS
    out_specs = [pl.BlockSpec((vt_rows, TP), lambda i: (0, i)) if o is None else row(o[0])
                 for o in outs]
    out_shape = [jax.ShapeDtypeStruct((vt_rows, ntok), BF16) if o is None
                 else jax.ShapeDtypeStruct((ntok, o[0]), o[1]) for o in outs]
    return pl.pallas_call(
        functools.partial(_proj_kernel, d=d, cw=cw, sgw=sgw),
        grid=(n_t,),
        in_specs=[
            row(d),
            pl.BlockSpec((None, None, 1, N_MOD * d), lambda i: (l, mod_row(i), 0, 0)),
            const(1, d),
            const(d, in_w),
            const(1, ATTN_WIDTH),
            const(1, KV_WIDTH),
            const(1, sgw),
            pl.BlockSpec((3, TP, LANES), lambda i: (0, rope_blk(i), 0)),
            pl.BlockSpec(e.shape, lambda i: (0, 0), pipeline_mode=pl.Buffered(1)),
        ],
        out_specs=out_specs,
        out_shape=out_shape,
        compiler_params=_cparams(1),
        name="proj",
    )(xc, mods, norm1, w_in, qg, kg, sgn, rope, e)


def _attn_kernel(q_ref, qn_ref, kl_ref, vl_ref, kc_ref, vc_ref, kn_ref, o_ref,
                 m_scr, acc_scr, sa_scr, sb_scr, sc_scr, ma_scr, mb_scr, mc_scr, *, n_lat, s_len):
    i = pl.program_id(0)
    n_steps = s_len // TK
    m_scr[...] = jnp.full(m_scr.shape, -jnp.inf, F32)
    acc_scr[...] = jnp.zeros(acc_scr.shape, F32)

    bufs = {"a": (sa_scr, ma_scr), "b": (sb_scr, mb_scr), "c": (sc_scr, mc_scr)}

    def produce(buf, k_tile, hd, qr=q_ref):
        slab, par, kvh = hd // 2, hd % 2, hd // GROUP
        qs = qr[:, slab * LANES:(slab + 1) * LANES]
        kcol = (2 * kvh + par) * LANES
        s = _dot_t(k_tile[:, kcol:kcol + LANES], qs)
        s_scr, mx_scr = bufs[buf]
        s_scr[hd] = s
        mx_scr[hd] = jnp.max(s, axis=0, keepdims=True)

    def consume(buf, hd, vt_tile):
        s_scr, mx_scr = bufs[buf]
        kvh = hd // GROUP
        m_prev = m_scr[hd]
        m_new = jnp.maximum(m_prev, mx_scr[hd])
        alpha = jnp.exp2(m_prev - m_new)
        p = jnp.exp2(s_scr[hd] - m_new).astype(BF16)
        pv = _dot(vt_tile[kvh * VT_ROWS:(kvh + 1) * VT_ROWS, :], p)
        acc_scr[hd] = acc_scr[hd] * alpha + pv
        m_scr[hd] = m_new

    def step(cur, vt_tile, next_k, nxt, next_q=q_ref):
        if next_k is not None:
            for hd in range(SKEW):
                produce(nxt, next_k, hd, next_q)
        for hd in range(N_Q_HEADS):
            if next_k is not None and hd + SKEW < N_Q_HEADS:
                produce(nxt, next_k, hd + SKEW, next_q)
            consume(cur, hd, vt_tile)

    def k_at(j):
        return kl_ref[pl.ds(pl.multiple_of(j * TK, TK), TK), :]

    def v_at(j):
        return vl_ref[:, pl.ds(pl.multiple_of(j * TK, TK), TK)]

    @pl.when(i == 0)
    def _():
        for hd in range(N_Q_HEADS):
            produce("a", k_at(0), hd)

    @pl.when(i < n_lat)
    def _():
        def body(jj, carry):
            for u in range(UNROLL):
                j = UNROLL * jj + u
                step("ab"[u % 2], v_at(j), k_at(j + 1), "ab"[(u + 1) % 2])
            return carry
        lax.fori_loop(0, n_steps // UNROLL - 1, body, 0)
        for u in range(UNROLL - 1):
            j = n_steps - UNROLL + u
            step("ab"[u % 2], v_at(j), k_at(j + 1), "ab"[(u + 1) % 2])
        step("b", v_at(n_steps - 1), kc_ref[...], "c")

    @pl.when(i >= n_lat)
    def _():
        for hd in range(N_Q_HEADS):
            produce("c", kc_ref[...], hd)

    @pl.when(i < n_lat - 1)
    def _():
        step("c", vc_ref[...], kn_ref[...], "a", qn_ref)

    @pl.when(i >= n_lat - 1)
    def _():
        step("c", vc_ref[...], None, None)

    for slab in range(N_Q_HEADS // 2):
        halves = []
        for hd in (2 * slab, 2 * slab + 1):
            a = acc_scr[hd]
            halves.append(a[0:HEAD_DIM, :] / a[HEAD_DIM:HEAD_DIM + 1, :])
        out = jnp.concatenate(halves, axis=0).T
        o_ref[:, slab * LANES:(slab + 1) * LANES] = out.astype(o_ref.dtype)


def _attn(q, kk, vt, geo, n_tiles):
    n_lat, n_s, n_l, bsz = geo["n_lat"], geo["n_s"], geo["n_l"], geo["B"]
    s_len, l_len = n_s * TM, n_l * TM
    ctx_blk0 = (bsz * s_len) // l_len
    vt_rows = N_KV_HEADS * VT_ROWS

    def lat_b(i):
        return jnp.minimum(i // n_s, bsz - 1)

    def ctx_b(i):
        return ctx_blk0 + jnp.where(i < n_lat, i // n_s, (i - n_lat) // n_l)

    def nxt(i):
        return jnp.minimum(i + 1, n_tiles - 1)

    return pl.pallas_call(
        functools.partial(_attn_kernel, n_lat=n_lat, s_len=s_len),
        grid=(n_tiles,),
        in_specs=[
            pl.BlockSpec((TM, ATTN_WIDTH), lambda i: (i, 0)),
            pl.BlockSpec((TM, ATTN_WIDTH), lambda i: (nxt(i), 0)),
            pl.BlockSpec((s_len, 4 * LANES), lambda i: (lat_b(i), 0)),
            pl.BlockSpec((vt_rows, s_len), lambda i: (0, lat_b(i))),
            pl.BlockSpec((l_len, 4 * LANES), lambda i: (ctx_b(i), 0)),
            pl.BlockSpec((vt_rows, l_len), lambda i: (0, ctx_b(i))),
            pl.BlockSpec((TK, 4 * LANES), lambda i: (lat_b(nxt(i)) * (s_len // TK), 0)),
        ],
        out_specs=pl.BlockSpec((TM, ATTN_WIDTH), lambda i: (i, 0)),
        out_shape=jax.ShapeDtypeStruct((n_tiles * TM, ATTN_WIDTH), BF16),
        scratch_shapes=[pltpu.VMEM((N_Q_HEADS, 1, TM), F32),
                        pltpu.VMEM((N_Q_HEADS, VT_ROWS, TM), F32),
                        pltpu.VMEM((N_Q_HEADS, TK, TM), F32),
                        pltpu.VMEM((N_Q_HEADS, TK, TM), F32),
                        pltpu.VMEM((N_Q_HEADS, l_len, TM), F32),
                        pltpu.VMEM((N_Q_HEADS, 1, TM), F32),
                        pltpu.VMEM((N_Q_HEADS, 1, TM), F32),
                        pltpu.VMEM((N_Q_HEADS, 1, TM), F32)],
        compiler_params=_cparams(1),
        name="attn",
    )(q, q, kk, vt, kk, vt, kk)


def _merge_ffn_kernel(x_ref, mod_ref, ab_ref, z_ref, zp_ref, zn_ref, at_ref, gu_ref, svn_ref,
                      gate_ref, cw_ref, ws_ref, bt_ref, wa_ref, wb_ref, wc_ref, wo_ref,
                      n2_ref, w1_ref, w3_ref, w2_ref, o_ref, *, d, n_lat, n_s, n_l, ff_chunk):
    i = pl.program_id(0)
    tm = x_ref.shape[0]
    t_in_seq = jnp.where(i < n_lat, i % n_s, (i - n_lat) % n_l)
    seq_tiles = jnp.where(i < n_lat, n_s, n_l)
    has_prev = (t_in_seq > 0).astype(F32)
    has_next = (t_in_seq < seq_tiles - 1).astype(F32)

    yb = _dot(at_ref[...], wb_ref[...])

    z = z_ref[...]
    row = lax.broadcasted_iota(jnp.int32, z.shape, 0)
    prev_row = zp_ref[SUBLANES - 1:SUBLANES, :] * has_prev
    next_row = zn_ref[0:1, :] * has_next
    zp = jnp.where(row == 0, prev_row, pltpu.roll(z, 1, axis=0))
    zn = jnp.where(row == tm - 1, next_row, pltpu.roll(z, tm - 1, axis=0))
    conv = zp * cw_ref[0:1, :] + z * cw_ref[1:2, :] + zn * cw_ref[2:3, :]
    ya = _dot((ab_ref[...] * conv).astype(BF16), wa_ref[...])

    sgw = svn_ref.shape[1]
    gw = sgw // SG_GROUPS
    lane = lax.broadcasted_iota(jnp.int32, (CHUNK, sgw), 1)
    parts = []
    for c in range(tm // CHUNK):
        vc = svn_ref[c * CHUNK:(c + 1) * CHUNK, :]
        stacked = jnp.concatenate(
            [jnp.where((lane >= g * gw) & (lane < (g + 1) * gw), vc, jnp.zeros_like(vc))
             for g in range(SG_GROUPS)], axis=0)
        mixed = _dot(ws_ref[...], stacked) + bt_ref[...]
        parts.append(gu_ref[c * CHUNK:(c + 1) * CHUNK, :] * mixed)
    yc = _dot(jnp.concatenate(parts, axis=0).astype(BF16), wc_ref[...])

    y = gate_ref[:, 0:d] * ya + gate_ref[:, d:2 * d] * yb + gate_ref[:, 2 * d:3 * d] * yc
    out = _dot(y.astype(BF16), wo_ref[...])
    x1 = x_ref[...] + mod_ref[:, 2 * d:3 * d] * out

    shift = mod_ref[:, 3 * d:4 * d]
    scale = mod_ref[:, 4 * d:5 * d]
    hb = ((_rms(x1) * n2_ref[...]) * (1.0 + scale) + shift).astype(BF16)
    d_ff = w1_ref.shape[1]
    acc = None
    for c in range(d_ff // ff_chunk):
        cs = slice(c * ff_chunk, (c + 1) * ff_chunk)
        a = _dot(hb, w1_ref[:, cs])
        b = _dot(hb, w3_ref[:, cs])
        act = ((a * jax.nn.sigmoid(a)) * b).astype(BF16)
        part = _dot(act, w2_ref[cs, :])
        acc = part if acc is None else acc + part
    o_ref[...] = x1 + mod_ref[:, 5 * d:6 * d] * acc


def _merge_ffn(xc, mods, l, ab, z, attn, gu, svn, gate, conv_w, w_s, bias_tab, w_a, w_b, w_c, w_o,
               norm2, w1, w3, w2, geo, n_tiles):
    d = xc.shape[1]
    cw = ab.shape[1]
    sgw = gu.shape[1]
    d_ff = w1.shape[-1]
    ff_chunk = d_ff // 2 if (d_ff // 2) % LANES == 0 else d_ff
    n_lat, n_s, n_l, bsz = geo["n_lat"], geo["n_s"], geo["n_l"], geo["B"]
    rows_per_tile = TM // SUBLANES
    last_blk = z.shape[0] // SUBLANES - 1

    def mod_row(i):
        return jnp.where(i < n_lat, i // n_s, bsz)

    row = lambda w: pl.BlockSpec((TM, w), lambda i: (i, 0))
    whole = lambda a, b: pl.BlockSpec((None, a, b), lambda i: (l, 0, 0),
                                      pipeline_mode=pl.Buffered(1))
    return pl.pallas_call(
        functools.partial(_merge_ffn_kernel, d=d, n_lat=n_lat, n_s=n_s, n_l=n_l,
                          ff_chunk=ff_chunk),
        grid=(n_tiles,),
        in_specs=[
            row(d),
            pl.BlockSpec((None, None, 1, N_MOD * d), lambda i: (l, mod_row(i), 0, 0)),
            row(cw),
            row(cw),
            pl.BlockSpec((SUBLANES, cw), lambda i: (jnp.maximum(i * rows_per_tile - 1, 0), 0)),
            pl.BlockSpec((SUBLANES, cw),
                         lambda i: (jnp.minimum((i + 1) * rows_per_tile, last_blk), 0)),
            row(ATTN_WIDTH),
            row(sgw),
            row(sgw),
            row(N_BRANCH * d),
            whole(CONV_K, cw),
            whole(CHUNK, SG_GROUPS * CHUNK),
            whole(CHUNK, sgw),
            whole(cw, d),
            whole(ATTN_WIDTH, d),
            whole(sgw, d),
            whole(d, d),
            whole(1, d),
            whole(d, d_ff),
            whole(d, d_ff),
            whole(d_ff, d),
        ],
        out_specs=row(d),
        out_shape=jax.ShapeDtypeStruct((n_tiles * TM, d), F32),
        compiler_params=_cparams(1),
        name="merge_ffn",
    )(xc, mods, ab, z, z, z, attn, gu, svn, gate, conv_w, w_s, bias_tab, w_a, w_b, w_c, w_o,
      norm2, w1, w3, w2)


def _rope_table(s_len, pad):
    t = jnp.arange(s_len, dtype=jnp.int32)
    pos = jnp.stack([(t // GRID_W).astype(F32), (t % GRID_W).astype(F32)], axis=1)
    inv_freq = ROPE_THETA ** (-jnp.arange(0, AXIS_DIM, 2, dtype=F32) / AXIS_DIM)
    ang = pos[:, :, None] * inv_freq
    cos, sin = jnp.cos(ang), jnp.sin(ang)
    zero = jnp.zeros_like(sin)
    c64 = jnp.stack([cos, cos], axis=2).reshape(s_len, HEAD_DIM)
    a64 = jnp.stack([-sin, zero], axis=2).reshape(s_len, HEAD_DIM)
    b64 = jnp.stack([zero, sin], axis=2).reshape(s_len, HEAD_DIM)
    tab = jnp.stack([c64, a64, b64], axis=0)
    ident = jnp.stack([jnp.ones((pad, HEAD_DIM), F32), jnp.zeros((pad, HEAD_DIM), F32),
                       jnp.zeros((pad, HEAD_DIM), F32)], axis=0)
    tab = jnp.concatenate([tab, ident], axis=1)
    return jnp.tile(tab, (1, 1, LANES // HEAD_DIM))


def _head_mean_matrix():
    r = jnp.arange(MXU_DIM) // HEAD_DIM
    return jnp.where(r[:, None] == r[None, :], 1.0 / HEAD_DIM, 0.0).astype(BF16)


def kernel(x, c, ctx, c_ctx, w_mod, b_mod, norm1, w_in, q_gain, k_gain, conv_w, sg_norm,
           w_s, b_s, w_a, w_b, w_c, w_o, norm2, w_ff1, w_ff3, w_ff2):
    bsz, s_len, d = x.shape
    l_len = ctx.shape[1]
    depth = w_mod.shape[0]
    sgw = sg_norm.shape[1]
    assert s_len % TM == 0 and l_len % TM == 0 and s_len % (UNROLL * TK) == 0 and s_len % GRID_W == 0
    assert (bsz * s_len) % l_len == 0 and bsz + 1 <= MOD_ROWS
    assert s_len % TP == 0 and (bsz * l_len) % TP == 0 and TP % TM == 0
    n_s, n_l = s_len // TM, l_len // TM
    geo = {"B": bsz, "n_s": n_s, "n_l": n_l, "n_lat": bsz * n_s}
    n_all = bsz * (n_s + n_l)

    cond = jnp.concatenate(
        [c, c_ctx[None, :], jnp.zeros((MOD_ROWS - bsz - 1, d), F32)], axis=0)
    mods = _adaln_all(cond, w_mod, b_mod).reshape(depth, MOD_ROWS, 1, N_MOD * d)

    rope = _rope_table(s_len, TP)
    e = _head_mean_matrix()
    qg = jnp.tile(q_gain, (1, N_Q_HEADS)).reshape(depth, 1, ATTN_WIDTH)
    kg = jnp.tile(k_gain, (1, N_KV_HEADS)).reshape(depth, 1, KV_WIDTH)
    sgn = sg_norm.reshape(depth, 1, sgw)
    n1 = norm1.reshape(depth, 1, d)
    n2 = norm2.reshape(depth, 1, d)
    bias_tab = jnp.repeat(jnp.swapaxes(b_s, 1, 2), sgw // SG_GROUPS, axis=2)
    w_in_b = w_in.astype(BF16)
    w_s_b = jnp.transpose(w_s, (0, 2, 1, 3)).reshape(depth, CHUNK, SG_GROUPS * CHUNK).astype(BF16)
    w_a_b, w_b_b, w_c_b, w_o_b = (w.astype(BF16) for w in (w_a, w_b, w_c, w_o))
    w1_b, w3_b, w2_b = (w.astype(BF16) for w in (w_ff1, w_ff3, w_ff2))

    xc = jnp.concatenate([x.reshape(bsz * s_len, d), ctx.reshape(bsz * l_len, d)], axis=0)
    for l in range(depth):
        n_tiles = geo["n_lat"] if l == depth - 1 else n_all
        ab, z, q, kk, vt, gu, svn, gate = _proj(xc, mods, l, n1, w_in_b, qg, kg, sgn, rope, e, geo)
        attn = _attn(q, kk, vt, geo, n_tiles)
        xc = _merge_ffn(xc, mods, l, ab, z, attn, gu, svn, gate, conv_w, w_s_b, bias_tab,
                        w_a_b, w_b_b, w_c_b, w_o_b, n2, w1_b, w3_b, w2_b, geo, n_tiles)
    return xc.reshape(bsz, s_len, d)
```

```python
import functools
import math

import jax
import jax.numpy as jnp
from jax import lax
from jax.experimental import pallas as pl
from jax.experimental.pallas import tpu as pltpu

F32 = jnp.float32
BF16 = jnp.bfloat16

GRID_W = 64
HEAD_DIM = 64
N_Q_HEADS = 8
N_KV_HEADS = 2
GROUP = N_Q_HEADS // N_KV_HEADS
ATTN_WIDTH = N_Q_HEADS * HEAD_DIM
KV_WIDTH = N_KV_HEADS * HEAD_DIM
AXIS_DIM = HEAD_DIM // 2
ROPE_THETA = 10000.0
CONV_K = 3
CHUNK = 128
SG_GROUPS = 4
N_BRANCH = 3
N_MOD = 6
EPS = 1e-6

LANES = 128
MXU_DIM = 256
SUBLANES = 8
TM = 256
TP = 512
TK = 512
ONES_ROWS = 16
VT_ROWS = HEAD_DIM + ONES_ROWS
Q_SCALE = HEAD_DIM ** -0.5 * math.log2(math.e)
SKEW = 2
UNROLL = 4
MOD_ROWS = 8
RIDER_ROWS = 512
N_PROJ_IN = 10
N_PROJ_OUT = 8
VMEM_LIMIT = 56 * 1024 * 1024


def _cparams(n_axes=1):
    return pltpu.CompilerParams(dimension_semantics=("arbitrary",) * n_axes,
                                vmem_limit_bytes=VMEM_LIMIT)


def _dot(a, b):
    return jnp.dot(a, b, preferred_element_type=F32)


def _dot_t(a, b):
    return lax.dot_general(a, b, (((1,), (1,)), ((), ())), preferred_element_type=F32)


def _mod_kernel(cond_ref, w_ref, b_ref, o_ref):
    cnd = cond_ref[...]
    s = (cnd * jax.nn.sigmoid(cnd)).astype(BF16)
    o_ref[...] = _dot(s, w_ref[...].astype(BF16)) + b_ref[...]


def _adaln_all(cond, w_mod, b_mod):
    depth, d, width = w_mod.shape
    tn = width // 4
    return pl.pallas_call(
        _mod_kernel,
        grid=(depth, width // tn),
        in_specs=[
            pl.BlockSpec((MOD_ROWS, d), lambda l, j: (0, 0)),
            pl.BlockSpec((None, d, tn), lambda l, j: (l, 0, j)),
            pl.BlockSpec((None, 1, tn), lambda l, j: (l, 0, j)),
        ],
        out_specs=pl.BlockSpec((None, MOD_ROWS, tn), lambda l, j: (l, 0, j)),
        out_shape=jax.ShapeDtypeStruct((depth, MOD_ROWS, width), F32),
        compiler_params=_cparams(2),
        name="adaln",
    )(cond, w_mod, b_mod.reshape(depth, 1, width))


def _rms(x, eps=EPS):
    return x * lax.rsqrt(jnp.mean(x * x, axis=-1, keepdims=True) + eps)


def _gelu_tanh(x):
    c = math.sqrt(2.0 / math.pi)
    return 0.5 * x * (1.0 + jnp.tanh(c * (x + 0.044715 * (x * x * x))))


def _head_mean_sq(p, e):
    sq = p * p
    hi = sq.astype(BF16)
    lo = (sq - hi.astype(F32)).astype(BF16)
    blk = min(p.shape[1], e.shape[0])
    eb = e[0:blk, 0:blk]
    cols = [_dot(hi[:, c:c + blk], eb) + _dot(lo[:, c:c + blk], eb)
            for c in range(0, p.shape[1], blk)]
    return cols[0] if len(cols) == 1 else jnp.concatenate(cols, axis=1)


def _rope_slab(t, cos, sa, sb):
    up = pltpu.roll(t, LANES - 16, axis=1)
    dn = pltpu.roll(t, 16, axis=1)
    return t * cos + up * sa + dn * sb


def _proj_kernel(*refs, d, cw, sgw, n_lat, n_rid, split):
    (xa_ref, xb_ref, mod_ref, n1_ref, w_ref, qg_ref, kg_ref, sgn_ref, rope_ref,
     e_ref) = refs[:N_PROJ_IN]
    rid_in = refs[N_PROJ_IN:N_PROJ_IN + n_rid]
    (ab_ref, z_ref, q_ref, kk_ref, vt_ref, gu_ref, svn_ref,
     gate_ref) = refs[N_PROJ_IN + n_rid:N_PROJ_IN + n_rid + N_PROJ_OUT]
    rid_out = refs[N_PROJ_IN + n_rid + N_PROJ_OUT:]
    off_q = 3 * cw
    off_k = off_q + ATTN_WIDTH
    off_v = off_k + KV_WIDTH
    off_u = off_v + KV_WIDTH
    off_g = off_u + 2 * sgw

    for src, dst in zip(rid_in, rid_out):
        dst[...] = src[...].astype(BF16)

    x = xa_ref[...]
    if split:
        x = jnp.where(pl.program_id(0) < n_lat, x, xb_ref[...])
    shift = mod_ref[:, 0:d]
    scale = mod_ref[:, d:2 * d]
    h = (_rms(x) * n1_ref[...]) * (1.0 + scale) + shift
    hb = h.astype(BF16)

    cos = rope_ref[0]
    sa = rope_ref[1]
    sb = rope_ref[2]
    e = e_ref[...]

    pq = _dot(hb, w_ref[:, off_q:off_k])
    qn = pq * lax.rsqrt(_head_mean_sq(pq, e) + EPS) * qg_ref[...]
    for s in range(ATTN_WIDTH // LANES):
        r = _rope_slab(qn[:, s * LANES:(s + 1) * LANES], cos, sa, sb)
        q_ref[:, s * LANES:(s + 1) * LANES] = (r * Q_SCALE).astype(BF16)

    pk = _dot(hb, w_ref[:, off_k:off_v])
    kn = pk * lax.rsqrt(_head_mean_sq(pk, e) + EPS) * kg_ref[...]
    kr = _rope_slab(kn, cos, sa, sb)
    ksw = pltpu.roll(kr, HEAD_DIM, axis=1)
    lane = lax.broadcasted_iota(jnp.int32, kr.shape, 1)
    lo_half = lane < HEAD_DIM
    zero = jnp.zeros_like(kr)
    kk_ref[:, 0 * LANES:1 * LANES] = jnp.where(lo_half, kr, zero).astype(BF16)
    kk_ref[:, 1 * LANES:2 * LANES] = jnp.where(lo_half, zero, ksw).astype(BF16)
    kk_ref[:, 2 * LANES:3 * LANES] = jnp.where(lo_half, ksw, zero).astype(BF16)
    kk_ref[:, 3 * LANES:4 * LANES] = jnp.where(lo_half, zero, kr).astype(BF16)

    pvt = _dot(hb, w_ref[:, off_v:off_u]).T.astype(BF16)
    ones = jnp.ones((ONES_ROWS, pvt.shape[1]), BF16)
    for hh in range(N_KV_HEADS):
        vt_ref[hh * VT_ROWS:hh * VT_ROWS + HEAD_DIM, :] = pvt[hh * HEAD_DIM:(hh + 1) * HEAD_DIM, :]
        vt_ref[hh * VT_ROWS + HEAD_DIM:(hh + 1) * VT_ROWS, :] = ones

    pu = _dot(hb, w_ref[:, off_u:off_g])
    gu_ref[...] = _gelu_tanh(pu[:, 0:sgw])
    svn_ref[...] = (_rms(_gelu_tanh(pu[:, sgw:2 * sgw])) * sgn_ref[...]).astype(BF16)

    for j in range(N_BRANCH):
        pg = _dot(hb, w_ref[:, off_g + j * d:off_g + (j + 1) * d])
        gate_ref[:, j * d:(j + 1) * d] = jax.nn.sigmoid(pg)

    pc = _dot(hb, w_ref[:, 0:off_q])
    ab_ref[...] = pc[:, 0:cw]
    z_ref[...] = pc[:, cw:2 * cw] * pc[:, 2 * cw:3 * cw]


def _proj(xa, xb, mods, l, norm1, w_in, qg, kg, sgn, rope, e, geo, riders):
    split = xb is not None
    d = xa.shape[1]
    in_w = w_in.shape[-1]
    cw = d // 4
    sgw = d // 4
    bsz = geo["B"]
    n_s = geo["n_s"] * TM // TP
    n_lat = bsz * n_s
    n_t = n_lat + bsz * geo["n_l"] * TM // TP
    ntok = n_t * TP
    n_blk = RIDER_ROWS // (2 * SUBLANES)
    while n_blk > n_t:
        n_blk //= 2
    rb = RIDER_ROWS // n_blk

    def mod_row(i):
        return jnp.where(i < n_lat, i // n_s, bsz)

    def rope_blk(i):
        return jnp.where(i < n_lat, i % n_s, n_s)

    def rid_blk(i):
        return jnp.minimum(i, n_blk - 1)

    row = lambda w: pl.BlockSpec((TP, w), lambda i: (i, 0))
    const = lambda *blk: pl.BlockSpec((None,) + blk, lambda i: (l,) + (0,) * len(blk),
                                      pipeline_mode=pl.Buffered(1))
    outs = [(cw, F32), (cw, F32), (ATTN_WIDTH, BF16), (4 * LANES, BF16), None,
            (sgw, F32), (sgw, BF16), (N_BRANCH * d, F32)]
    vt_rows = N_KV_HEADS * VT_ROWS
    out_specs = [pl.BlockSpec((vt_rows, TP), lambda i: (0, i)) if o is None else row(o[0])
                 for o in outs]
    out_shape = [jax.ShapeDtypeStruct((vt_rows, ntok), BF16) if o is None
                 else jax.ShapeDtypeStruct((ntok, o[0]), o[1]) for o in outs]
    rid_in_specs = [pl.BlockSpec((None, rb, a.shape[2]), lambda i, li=li: (li, rid_blk(i), 0))
                    for a, li in riders]
    out_specs += [pl.BlockSpec((rb, a.shape[2]), lambda i: (rid_blk(i), 0)) for a, _ in riders]
    out_shape += [jax.ShapeDtypeStruct((RIDER_ROWS, a.shape[2]), BF16) for a, _ in riders]
    res = pl.pallas_call(
        functools.partial(_proj_kernel, d=d, cw=cw, sgw=sgw, n_lat=n_lat, n_rid=len(riders),
                          split=split),
        grid=(n_t,),
        in_specs=[
            pl.BlockSpec((TP, d), lambda i: (jnp.minimum(i, n_lat - 1) if split else i, 0)),
            pl.BlockSpec((TP, d), lambda i: (jnp.maximum(i - n_lat, 0) if split else 0, 0)),
            pl.BlockSpec((None, None, 1, N_MOD * d), lambda i: (l, mod_row(i), 0, 0)),
            const(1, d),
            pl.BlockSpec((d, in_w), lambda i: (0, 0), pipeline_mode=pl.Buffered(1)),
            const(1, ATTN_WIDTH),
            const(1, KV_WIDTH),
            const(1, sgw),
            pl.BlockSpec((3, TP, LANES), lambda i: (0, rope_blk(i), 0)),
            pl.BlockSpec(e.shape, lambda i: (0, 0), pipeline_mode=pl.Buffered(1)),
        ] + rid_in_specs,
        out_specs=out_specs,
        out_shape=out_shape,
        compiler_params=_cparams(1),
        name="proj",
    )(xa, xb if split else xa, mods, norm1, w_in, qg, kg, sgn, rope, e, *[a for a, _ in riders])
    return res[:N_PROJ_OUT], res[N_PROJ_OUT:]


def _attn_kernel(q_ref, qn_ref, kl_ref, vl_ref, kc_ref, vc_ref, kn_ref, o_ref,
                 m_scr, acc_scr, sa_scr, sb_scr, sc_scr, ma_scr, mb_scr, mc_scr, *, n_lat, s_len):
    i = pl.program_id(0)
    n_steps = s_len // TK
    m_scr[...] = jnp.full(m_scr.shape, -jnp.inf, F32)
    acc_scr[...] = jnp.zeros(acc_scr.shape, F32)

    bufs = {"a": (sa_scr, ma_scr), "b": (sb_scr, mb_scr), "c": (sc_scr, mc_scr)}

    def produce(buf, k_tile, hd, qr=q_ref):
        slab, par, kvh = hd // 2, hd % 2, hd // GROUP
        qs = qr[:, slab * LANES:(slab + 1) * LANES]
        kcol = (2 * kvh + par) * LANES
        s = _dot_t(k_tile[:, kcol:kcol + LANES], qs)
        s_scr, mx_scr = bufs[buf]
        s_scr[hd] = s
        mx_scr[hd] = jnp.max(s, axis=0, keepdims=True)

    def consume(buf, hd, vt_tile):
        s_scr, mx_scr = bufs[buf]
        kvh = hd // GROUP
        m_prev = m_scr[hd]
        m_new = jnp.maximum(m_prev, mx_scr[hd])
        alpha = jnp.exp2(m_prev - m_new)
        p = jnp.exp2(s_scr[hd] - m_new).astype(BF16)
        pv = _dot(vt_tile[kvh * VT_ROWS:(kvh + 1) * VT_ROWS, :], p)
        acc_scr[hd] = acc_scr[hd] * alpha + pv
        m_scr[hd] = m_new

    def step(cur, vt_tile, next_k, nxt, next_q=q_ref):
        if next_k is not None:
            for hd in range(SKEW):
                produce(nxt, next_k, hd, next_q)
        for hd in range(N_Q_HEADS):
            if next_k is not None and hd + SKEW < N_Q_HEADS:
                produce(nxt, next_k, hd + SKEW, next_q)
            consume(cur, hd, vt_tile)

    def k_at(j):
        return kl_ref[pl.ds(pl.multiple_of(j * TK, TK), TK), :]

    def v_at(j):
        return vl_ref[:, pl.ds(pl.multiple_of(j * TK, TK), TK)]

    @pl.when(i == 0)
    def _():
        for hd in range(N_Q_HEADS):
            produce("a", k_at(0), hd)

    @pl.when(i < n_lat)
    def _():
        def body(jj, carry):
            for u in range(UNROLL):
                j = UNROLL * jj + u
                step("ab"[u % 2], v_at(j), k_at(j + 1), "ab"[(u + 1) % 2])
            return carry
        lax.fori_loop(0, n_steps // UNROLL - 1, body, 0)
        for u in range(UNROLL - 1):
            j = n_steps - UNROLL + u
            step("ab"[u % 2], v_at(j), k_at(j + 1), "ab"[(u + 1) % 2])
        step("b", v_at(n_steps - 1), kc_ref[...], "c")

    @pl.when(i >= n_lat)
    def _():
        for hd in range(N_Q_HEADS):
            produce("c", kc_ref[...], hd)

    @pl.when(i < n_lat - 1)
    def _():
        step("c", vc_ref[...], kn_ref[...], "a", qn_ref)

    @pl.when(i >= n_lat - 1)
    def _():
        step("c", vc_ref[...], None, None)

    for slab in range(N_Q_HEADS // 2):
        halves = []
        for hd in (2 * slab, 2 * slab + 1):
            a = acc_scr[hd]
            halves.append(a[0:HEAD_DIM, :] / a[HEAD_DIM:HEAD_DIM + 1, :])
        out = jnp.concatenate(halves, axis=0).T
        o_ref[:, slab * LANES:(slab + 1) * LANES] = out.astype(o_ref.dtype)


def _attn(q, kk, vt, geo, n_tiles):
    n_lat, n_s, n_l, bsz = geo["n_lat"], geo["n_s"], geo["n_l"], geo["B"]
    s_len, l_len = n_s * TM, n_l * TM
    ctx_blk0 = (bsz * s_len) // l_len
    vt_rows = N_KV_HEADS * VT_ROWS

    def lat_b(i):
        return jnp.minimum(i // n_s, bsz - 1)

    def ctx_b(i):
        return ctx_blk0 + jnp.where(i < n_lat, i // n_s, (i - n_lat) // n_l)

    def nxt(i):
        return jnp.minimum(i + 1, n_tiles - 1)

    return pl.pallas_call(
        functools.partial(_attn_kernel, n_lat=n_lat, s_len=s_len),
        grid=(n_tiles,),
        in_specs=[
            pl.BlockSpec((TM, ATTN_WIDTH), lambda i: (i, 0)),
            pl.BlockSpec((TM, ATTN_WIDTH), lambda i: (nxt(i), 0)),
            pl.BlockSpec((s_len, 4 * LANES), lambda i: (lat_b(i), 0)),
            pl.BlockSpec((vt_rows, s_len), lambda i: (0, lat_b(i))),
            pl.BlockSpec((l_len, 4 * LANES), lambda i: (ctx_b(i), 0)),
            pl.BlockSpec((vt_rows, l_len), lambda i: (0, ctx_b(i))),
            pl.BlockSpec((TK, 4 * LANES), lambda i: (lat_b(nxt(i)) * (s_len // TK), 0)),
        ],
        out_specs=pl.BlockSpec((TM, ATTN_WIDTH), lambda i: (i, 0)),
        out_shape=jax.ShapeDtypeStruct((n_tiles * TM, ATTN_WIDTH), BF16),
        scratch_shapes=[pltpu.VMEM((N_Q_HEADS, 1, TM), F32),
                        pltpu.VMEM((N_Q_HEADS, VT_ROWS, TM), F32),
                        pltpu.VMEM((N_Q_HEADS, TK, TM), F32),
                        pltpu.VMEM((N_Q_HEADS, TK, TM), F32),
                        pltpu.VMEM((N_Q_HEADS, l_len, TM), F32),
                        pltpu.VMEM((N_Q_HEADS, 1, TM), F32),
                        pltpu.VMEM((N_Q_HEADS, 1, TM), F32),
                        pltpu.VMEM((N_Q_HEADS, 1, TM), F32)],
        compiler_params=_cparams(1),
        name="attn",
    )(q, q, kk, vt, kk, vt, kk)


def _merge_ffn_kernel(xa_ref, xb_ref, mod_ref, ab_ref, z_ref, zp_ref, zn_ref, at_ref, gu_ref, svn_ref,
                      gate_ref, cw_ref, ws_ref, bt_ref, wa_ref, wb_ref, wc_ref, wo_ref,
                      n2_ref, w1_ref, w3_ref, w2_ref, o_ref,
                      *, d, n_lat, n_s, n_l, ff_chunk, split):
    i = pl.program_id(0)
    tm = xa_ref.shape[0]
    t_in_seq = jnp.where(i < n_lat, i % n_s, (i - n_lat) % n_l)
    seq_tiles = jnp.where(i < n_lat, n_s, n_l)
    has_prev = (t_in_seq > 0).astype(F32)
    has_next = (t_in_seq < seq_tiles - 1).astype(F32)

    yb = _dot(at_ref[...], wb_ref[...])

    z = z_ref[...]
    row = lax.broadcasted_iota(jnp.int32, z.shape, 0)
    prev_row = zp_ref[SUBLANES - 1:SUBLANES, :] * has_prev
    next_row = zn_ref[0:1, :] * has_next
    zp = jnp.where(row == 0, prev_row, pltpu.roll(z, 1, axis=0))
    zn = jnp.where(row == tm - 1, next_row, pltpu.roll(z, tm - 1, axis=0))
    conv = zp * cw_ref[0:1, :] + z * cw_ref[1:2, :] + zn * cw_ref[2:3, :]
    ya = _dot((ab_ref[...] * conv).astype(BF16), wa_ref[...])

    sgw = svn_ref.shape[1]
    gw = sgw // SG_GROUPS
    lane = lax.broadcasted_iota(jnp.int32, (CHUNK, sgw), 1)
    parts = []
    for c in range(tm // CHUNK):
        vc = svn_ref[c * CHUNK:(c + 1) * CHUNK, :]
        stacked = jnp.concatenate(
            [jnp.where((lane >= g * gw) & (lane < (g + 1) * gw), vc, jnp.zeros_like(vc))
             for g in range(SG_GROUPS)], axis=0)
        mixed = _dot(ws_ref[...], stacked) + bt_ref[...]
        parts.append(gu_ref[c * CHUNK:(c + 1) * CHUNK, :] * mixed)
    yc = _dot(jnp.concatenate(parts, axis=0).astype(BF16), wc_ref[...])

    y = gate_ref[:, 0:d] * ya + gate_ref[:, d:2 * d] * yb + gate_ref[:, 2 * d:3 * d] * yc
    out = _dot(y.astype(BF16), wo_ref[...])
    x0 = xa_ref[...]
    if split:
        x0 = jnp.where(i < n_lat, x0, xb_ref[...])
    x1 = x0 + mod_ref[:, 2 * d:3 * d] * out

    shift = mod_ref[:, 3 * d:4 * d]
    scale = mod_ref[:, 4 * d:5 * d]
    hb = ((_rms(x1) * n2_ref[...]) * (1.0 + scale) + shift).astype(BF16)
    d_ff = w1_ref.shape[1]
    acc = None
    for c in range(d_ff // ff_chunk):
        cs = slice(c * ff_chunk, (c + 1) * ff_chunk)
        a = _dot(hb, w1_ref[:, cs])
        b = _dot(hb, w3_ref[:, cs])
        act = ((a * jax.nn.sigmoid(a)) * b).astype(BF16)
        part = _dot(act, w2_ref[cs, :])
        acc = part if acc is None else acc + part
    o_ref[...] = x1 + mod_ref[:, 5 * d:6 * d] * acc


def _merge_ffn(xa, xb, mods, l, ab, z, attn, gu, svn, gate, conv_w, w_s, bias_tab,
               w_a, w_b, w_c, w_o, norm2, w1, w3, w2, geo, n_tiles):
    split = xb is not None
    d = xa.shape[1]
    cw = ab.shape[1]
    sgw = gu.shape[1]
    d_ff = w1.shape[-1]
    ff_chunk = d_ff // 2 if (d_ff // 2) % LANES == 0 else d_ff
    n_lat, n_s, n_l, bsz = geo["n_lat"], geo["n_s"], geo["n_l"], geo["B"]
    rows_per_tile = TM // SUBLANES
    last_blk = z.shape[0] // SUBLANES - 1

    def mod_row(i):
        return jnp.where(i < n_lat, i // n_s, bsz)

    row = lambda w: pl.BlockSpec((TM, w), lambda i: (i, 0))
    stacked = lambda a, b: pl.BlockSpec((None, a, b), lambda i: (l, 0, 0),
                                        pipeline_mode=pl.Buffered(1))
    whole = lambda a, b: pl.BlockSpec((a, b), lambda i: (0, 0), pipeline_mode=pl.Buffered(1))
    return pl.pallas_call(
        functools.partial(_merge_ffn_kernel, d=d, n_lat=n_lat, n_s=n_s, n_l=n_l,
                          ff_chunk=ff_chunk, split=split),
        grid=(n_tiles,),
        in_specs=[
            pl.BlockSpec((TM, d), lambda i: (jnp.minimum(i, n_lat - 1) if split else i, 0)),
            pl.BlockSpec((TM, d), lambda i: (jnp.maximum(i - n_lat, 0) if split else 0, 0)),
            pl.BlockSpec((None, None, 1, N_MOD * d), lambda i: (l, mod_row(i), 0, 0)),
            row(cw),
            row(cw),
            pl.BlockSpec((SUBLANES, cw), lambda i: (jnp.maximum(i * rows_per_tile - 1, 0), 0)),
            pl.BlockSpec((SUBLANES, cw),
                         lambda i: (jnp.minimum((i + 1) * rows_per_tile, last_blk), 0)),
            row(ATTN_WIDTH),
            row(sgw),
            row(sgw),
            row(N_BRANCH * d),
            stacked(CONV_K, cw),
            stacked(CHUNK, SG_GROUPS * CHUNK),
            stacked(CHUNK, sgw),
            whole(cw, d),
            whole(ATTN_WIDTH, d),
            whole(sgw, d),
            whole(d, d),
            stacked(1, d),
            whole(d, d_ff),
            whole(d, d_ff),
            whole(d_ff, d),
        ],
        out_specs=row(d),
        out_shape=jax.ShapeDtypeStruct((n_tiles * TM, d), F32),
        compiler_params=_cparams(1),
        name="merge_ffn",
    )(xa, xb if split else xa, mods, ab, z, z, z, attn, gu, svn, gate, conv_w, w_s, bias_tab,
      w_a, w_b, w_c, w_o, norm2, w1, w3, w2)


def _rope_table(s_len, pad):
    t = jnp.arange(s_len, dtype=jnp.int32)
    pos = jnp.stack([(t // GRID_W).astype(F32), (t % GRID_W).astype(F32)], axis=1)
    inv_freq = ROPE_THETA ** (-jnp.arange(0, AXIS_DIM, 2, dtype=F32) / AXIS_DIM)
    ang = pos[:, :, None] * inv_freq
    cos, sin = jnp.cos(ang), jnp.sin(ang)
    zero = jnp.zeros_like(sin)
    c64 = jnp.stack([cos, cos], axis=2).reshape(s_len, HEAD_DIM)
    a64 = jnp.stack([-sin, zero], axis=2).reshape(s_len, HEAD_DIM)
    b64 = jnp.stack([zero, sin], axis=2).reshape(s_len, HEAD_DIM)
    tab = jnp.stack([c64, a64, b64], axis=0)
    ident = jnp.stack([jnp.ones((pad, HEAD_DIM), F32), jnp.zeros((pad, HEAD_DIM), F32),
                       jnp.zeros((pad, HEAD_DIM), F32)], axis=0)
    tab = jnp.concatenate([tab, ident], axis=1)
    return jnp.tile(tab, (1, 1, LANES // HEAD_DIM))


def _head_mean_matrix():
    r = jnp.arange(MXU_DIM) // HEAD_DIM
    return jnp.where(r[:, None] == r[None, :], 1.0 / HEAD_DIM, 0.0).astype(BF16)


def kernel(x, c, ctx, c_ctx, w_mod, b_mod, norm1, w_in, q_gain, k_gain, conv_w, sg_norm,
           w_s, b_s, w_a, w_b, w_c, w_o, norm2, w_ff1, w_ff3, w_ff2):
    bsz, s_len, d = x.shape
    l_len = ctx.shape[1]
    depth = w_mod.shape[0]
    sgw = sg_norm.shape[1]
    assert s_len % TM == 0 and l_len % TM == 0 and s_len % (UNROLL * TK) == 0 and s_len % GRID_W == 0
    assert (bsz * s_len) % l_len == 0 and bsz + 1 <= MOD_ROWS
    assert s_len % TP == 0 and (bsz * l_len) % TP == 0 and TP % TM == 0
    n_s, n_l = s_len // TM, l_len // TM
    geo = {"B": bsz, "n_s": n_s, "n_l": n_l, "n_lat": bsz * n_s}
    n_all = bsz * (n_s + n_l)

    cond = jnp.concatenate(
        [c, c_ctx[None, :], jnp.zeros((MOD_ROWS - bsz - 1, d), F32)], axis=0)
    mods = _adaln_all(cond, w_mod, b_mod).reshape(depth, MOD_ROWS, 1, N_MOD * d)

    rope = _rope_table(s_len, TP)
    e = _head_mean_matrix()
    qg = jnp.tile(q_gain, (1, N_Q_HEADS)).reshape(depth, 1, ATTN_WIDTH)
    kg = jnp.tile(k_gain, (1, N_KV_HEADS)).reshape(depth, 1, KV_WIDTH)
    sgn = sg_norm.reshape(depth, 1, sgw)
    n1 = norm1.reshape(depth, 1, d)
    n2 = norm2.reshape(depth, 1, d)
    bias_tab = jnp.repeat(jnp.swapaxes(b_s, 1, 2), sgw // SG_GROUPS, axis=2)
    w_s_b = jnp.transpose(w_s, (0, 2, 1, 3)).reshape(depth, CHUNK, SG_GROUPS * CHUNK).astype(BF16)
    flat = lambda w: w.reshape(depth, RIDER_ROWS, -1)
    later = [w_a, w_b, w_c, w_o, w_ff1, w_ff3, w_ff2]
    w_in_l = w_in[0].astype(BF16)

    xa, xb = x.reshape(bsz * s_len, d), ctx.reshape(bsz * l_len, d)
    for l in range(depth):
        last = l == depth - 1
        n_tiles = geo["n_lat"] if last else n_all
        riders = [(flat(w), l) for w in later] + ([] if last else [(flat(w_in), l + 1)])
        (ab, z, q, kk, vt, gu, svn, gate), cast = _proj(
            xa, xb, mods, l, n1, w_in_l, qg, kg, sgn, rope, e, geo, riders)
        wa_l, wb_l, wc_l, wo_l, w1_l, w3_l, w2_l = (
            cw_.reshape(w.shape[1:]) for cw_, w in zip(cast, later))
        if not last:
            w_in_l = cast[len(later)].reshape(w_in.shape[1:])
        attn = _attn(q, kk, vt, geo, n_tiles)
        xa = _merge_ffn(xa, xb, mods, l, ab, z, attn, gu, svn, gate, conv_w, w_s_b,
                        bias_tab, wa_l, wb_l, wc_l, wo_l, n2, w1_l, w3_l, w2_l, geo, n_tiles)
        xb = None
    return xa.reshape(bsz, s_len, d)
```

```python
import functools
import math

import jax
import jax.numpy as jnp
from jax import lax
from jax.experimental import pallas as pl
from jax.experimental.pallas import tpu as pltpu

F32 = jnp.float32
BF16 = jnp.bfloat16

GRID_W = 64
HEAD_DIM = 64
N_Q_HEADS = 8
N_KV_HEADS = 2
GROUP = N_Q_HEADS // N_KV_HEADS
ATTN_WIDTH = N_Q_HEADS * HEAD_DIM
KV_WIDTH = N_KV_HEADS * HEAD_DIM
AXIS_DIM = HEAD_DIM // 2
ROPE_THETA = 10000.0
CONV_K = 3
CHUNK = 128
SG_GROUPS = 4
N_BRANCH = 3
N_MOD = 6
EPS = 1e-6

LANES = 128
MXU_DIM = 256
SUBLANES = 8
TM = 256
TP = 512
TK = 512
ONES_ROWS = 16
VT_ROWS = HEAD_DIM + ONES_ROWS
Q_SCALE = HEAD_DIM ** -0.5 * math.log2(math.e)
SKEW = 2
UNROLL = 4
MOD_ROWS = 8
N_PROJ_IN = 10
N_PROJ_OUT = 8
VMEM_LIMIT = 56 * 1024 * 1024


def _cparams(n_axes=1):
    return pltpu.CompilerParams(dimension_semantics=("arbitrary",) * n_axes,
                                vmem_limit_bytes=VMEM_LIMIT)


def _dot(a, b):
    return jnp.dot(a, b, preferred_element_type=F32)


def _dot_t(a, b):
    return lax.dot_general(a, b, (((1,), (1,)), ((), ())), preferred_element_type=F32)


def _mod_kernel(cond_ref, w_ref, b_ref, o_ref):
    cnd = cond_ref[...]
    s = (cnd * jax.nn.sigmoid(cnd)).astype(BF16)
    o_ref[...] = _dot(s, w_ref[...].astype(BF16)) + b_ref[...]


def _adaln_all(cond, w_mod, b_mod):
    depth, d, width = w_mod.shape
    tn = width // 4
    return pl.pallas_call(
        _mod_kernel,
        grid=(depth, width // tn),
        in_specs=[
            pl.BlockSpec((MOD_ROWS, d), lambda l, j: (0, 0)),
            pl.BlockSpec((None, d, tn), lambda l, j: (l, 0, j)),
            pl.BlockSpec((None, 1, tn), lambda l, j: (l, 0, j)),
        ],
        out_specs=pl.BlockSpec((None, MOD_ROWS, tn), lambda l, j: (l, 0, j)),
        out_shape=jax.ShapeDtypeStruct((depth, MOD_ROWS, width), F32),
        compiler_params=_cparams(2),
        name="adaln",
    )(cond, w_mod, b_mod.reshape(depth, 1, width))


def _rms(x, eps=EPS):
    return x * lax.rsqrt(jnp.mean(x * x, axis=-1, keepdims=True) + eps)


def _gelu_tanh(x):
    c = math.sqrt(2.0 / math.pi)
    return 0.5 * x * (1.0 + jnp.tanh(c * (x + 0.044715 * (x * x * x))))


def _head_mean_sq(p, e):
    sq = p * p
    hi = sq.astype(BF16)
    lo = (sq - hi.astype(F32)).astype(BF16)
    blk = min(p.shape[1], e.shape[0])
    eb = e[0:blk, 0:blk]
    cols = [_dot(hi[:, c:c + blk], eb) + _dot(lo[:, c:c + blk], eb)
            for c in range(0, p.shape[1], blk)]
    return cols[0] if len(cols) == 1 else jnp.concatenate(cols, axis=1)


def _rope_slab(t, cos, sa, sb):
    up = pltpu.roll(t, LANES - 16, axis=1)
    dn = pltpu.roll(t, 16, axis=1)
    return t * cos + up * sa + dn * sb


def _proj_kernel(*refs, d, cw, sgw, n_lat, n_rid, split):
    (xa_ref, xb_ref, mod_ref, n1_ref, w_ref, qg_ref, kg_ref, sgn_ref, rope_ref,
     e_ref) = refs[:N_PROJ_IN]
    rid_in = refs[N_PROJ_IN:N_PROJ_IN + n_rid]
    (ab_ref, z_ref, q_ref, kk_ref, vt_ref, gu_ref, svn_ref,
     gate_ref) = refs[N_PROJ_IN + n_rid:N_PROJ_IN + n_rid + N_PROJ_OUT]
    rid_out = refs[N_PROJ_IN + n_rid + N_PROJ_OUT:]
    off_q = 3 * cw
    off_k = off_q + ATTN_WIDTH
    off_v = off_k + KV_WIDTH
    off_u = off_v + KV_WIDTH
    off_g = off_u + 2 * sgw

    for src, dst in zip(rid_in, rid_out):
        dst[...] = src[...].astype(BF16)

    x = xa_ref[...]
    if split:
        x = jnp.where(pl.program_id(0) < n_lat, x, xb_ref[...])
    shift = mod_ref[:, 0:d]
    scale = mod_ref[:, d:2 * d]
    h = (_rms(x) * n1_ref[...]) * (1.0 + scale) + shift
    hb = h.astype(BF16)

    cos = rope_ref[0]
    sa = rope_ref[1]
    sb = rope_ref[2]
    e = e_ref[...]

    pq = _dot(hb, w_ref[:, off_q:off_k])
    qn = pq * lax.rsqrt(_head_mean_sq(pq, e) + EPS) * qg_ref[...]
    for s in range(ATTN_WIDTH // LANES):
        r = _rope_slab(qn[:, s * LANES:(s + 1) * LANES], cos, sa, sb)
        q_ref[:, s * LANES:(s + 1) * LANES] = (r * Q_SCALE).astype(BF16)

    pk = _dot(hb, w_ref[:, off_k:off_v])
    kn = pk * lax.rsqrt(_head_mean_sq(pk, e) + EPS) * kg_ref[...]
    kr = _rope_slab(kn, cos, sa, sb)
    ksw = pltpu.roll(kr, HEAD_DIM, axis=1)
    lane = lax.broadcasted_iota(jnp.int32, kr.shape, 1)
    lo_half = lane < HEAD_DIM
    zero = jnp.zeros_like(kr)
    kk_ref[:, 0 * LANES:1 * LANES] = jnp.where(lo_half, kr, zero).astype(BF16)
    kk_ref[:, 1 * LANES:2 * LANES] = jnp.where(lo_half, zero, ksw).astype(BF16)
    kk_ref[:, 2 * LANES:3 * LANES] = jnp.where(lo_half, ksw, zero).astype(BF16)
    kk_ref[:, 3 * LANES:4 * LANES] = jnp.where(lo_half, zero, kr).astype(BF16)

    pvt = _dot(hb, w_ref[:, off_v:off_u]).T.astype(BF16)
    ones = jnp.ones((ONES_ROWS, pvt.shape[1]), BF16)
    for hh in range(N_KV_HEADS):
        vt_ref[hh * VT_ROWS:hh * VT_ROWS + HEAD_DIM, :] = pvt[hh * HEAD_DIM:(hh + 1) * HEAD_DIM, :]
        vt_ref[hh * VT_ROWS + HEAD_DIM:(hh + 1) * VT_ROWS, :] = ones

    pu = _dot(hb, w_ref[:, off_u:off_g])
    gu_ref[...] = _gelu_tanh(pu[:, 0:sgw])
    svn_ref[...] = (_rms(_gelu_tanh(pu[:, sgw:2 * sgw])) * sgn_ref[...]).astype(BF16)

    for j in range(N_BRANCH):
        pg = _dot(hb, w_ref[:, off_g + j * d:off_g + (j + 1) * d])
        gate_ref[:, j * d:(j + 1) * d] = jax.nn.sigmoid(pg)

    pc = _dot(hb, w_ref[:, 0:off_q])
    ab_ref[...] = pc[:, 0:cw]
    z_ref[...] = pc[:, cw:2 * cw] * pc[:, 2 * cw:3 * cw]


def _proj(xa, xb, mods, l, norm1, w_in, qg, kg, sgn, rope, e, geo, riders):
    split = xb is not None
    d = xa.shape[1]
    in_w = w_in.shape[-1]
    cw = d // 4
    sgw = d // 4
    bsz = geo["B"]
    n_s = geo["n_s"] * TM // TP
    n_lat = bsz * n_s
    n_t = n_lat + bsz * geo["n_l"] * TM // TP
    ntok = n_t * TP

    def rider_blocks(rows):
        nb = 1
        while 2 * nb <= n_t and rows % (2 * nb * 2 * SUBLANES) == 0:
            nb *= 2
        return nb

    n_blks = [rider_blocks(a.shape[1]) for a, _ in riders]

    def mod_row(i):
        return jnp.where(i < n_lat, i // n_s, bsz)

    def rope_blk(i):
        return jnp.where(i < n_lat, i % n_s, n_s)


    row = lambda w: pl.BlockSpec((TP, w), lambda i: (i, 0))
    const = lambda *blk: pl.BlockSpec((None,) + blk, lambda i: (l,) + (0,) * len(blk),
                                      pipeline_mode=pl.Buffered(1))
    outs = [(cw, F32), (cw, F32), (ATTN_WIDTH, BF16), (4 * LANES, BF16), None,
            (sgw, F32), (sgw, BF16), (N_BRANCH * d, F32)]
    vt_rows = N_KV_HEADS * VT_ROWS
    out_specs = [pl.BlockSpec((vt_rows, TP), lambda i: (0, i)) if o is None else row(o[0])
                 for o in outs]
    out_shape = [jax.ShapeDtypeStruct((vt_rows, ntok), BF16) if o is None
                 else jax.ShapeDtypeStruct((ntok, o[0]), o[1]) for o in outs]
    rid_in_specs = [pl.BlockSpec((None, a.shape[1] // nb, a.shape[2]),
                                 lambda i, li=li, nb=nb: (li, jnp.minimum(i, nb - 1), 0))
                    for (a, li), nb in zip(riders, n_blks)]
    out_specs += [pl.BlockSpec((a.shape[1] // nb, a.shape[2]),
                               lambda i, nb=nb: (jnp.minimum(i, nb - 1), 0))
                  for (a, _), nb in zip(riders, n_blks)]
    out_shape += [jax.ShapeDtypeStruct(a.shape[1:], BF16) for a, _ in riders]
    res = pl.pallas_call(
        functools.partial(_proj_kernel, d=d, cw=cw, sgw=sgw, n_lat=n_lat, n_rid=len(riders),
                          split=split),
        grid=(n_t,),
        in_specs=[
            pl.BlockSpec((TP, d), lambda i: (jnp.minimum(i, n_lat - 1) if split else i, 0)),
            pl.BlockSpec((TP, d), lambda i: (jnp.maximum(i - n_lat, 0) if split else 0, 0)),
            pl.BlockSpec((None, None, 1, N_MOD * d), lambda i: (l, mod_row(i), 0, 0)),
            const(1, d),
            pl.BlockSpec((d, in_w), lambda i: (0, 0), pipeline_mode=pl.Buffered(1)),
            const(1, ATTN_WIDTH),
            const(1, KV_WIDTH),
            const(1, sgw),
            pl.BlockSpec((3, TP, LANES), lambda i: (0, rope_blk(i), 0)),
            pl.BlockSpec(e.shape, lambda i: (0, 0), pipeline_mode=pl.Buffered(1)),
        ] + rid_in_specs,
        out_specs=out_specs,
        out_shape=out_shape,
        compiler_params=_cparams(1),
        name="proj",
    )(xa, xb if split else xa, mods, norm1, w_in, qg, kg, sgn, rope, e, *[a for a, _ in riders])
    return res[:N_PROJ_OUT], res[N_PROJ_OUT:]


def _attn_kernel(q_ref, qn_ref, kl_ref, vl_ref, kc_ref, vc_ref, kn_ref, o_ref,
                 m_scr, acc_scr, sa_scr, sb_scr, sc_scr, ma_scr, mb_scr, mc_scr, *, n_lat, s_len):
    i = pl.program_id(0)
    n_steps = s_len // TK
    m_scr[...] = jnp.full(m_scr.shape, -jnp.inf, F32)
    acc_scr[...] = jnp.zeros(acc_scr.shape, F32)

    bufs = {"a": (sa_scr, ma_scr), "b": (sb_scr, mb_scr), "c": (sc_scr, mc_scr)}

    def produce(buf, k_tile, hd, qr=q_ref):
        slab, par, kvh = hd // 2, hd % 2, hd // GROUP
        qs = qr[:, slab * LANES:(slab + 1) * LANES]
        kcol = (2 * kvh + par) * LANES
        s = _dot_t(k_tile[:, kcol:kcol + LANES], qs)
        s_scr, mx_scr = bufs[buf]
        s_scr[hd] = s
        mx_scr[hd] = jnp.max(s, axis=0, keepdims=True)

    def consume(buf, hd, vt_tile):
        s_scr, mx_scr = bufs[buf]
        kvh = hd // GROUP
        m_prev = m_scr[hd]
        m_new = jnp.maximum(m_prev, mx_scr[hd])
        alpha = jnp.exp2(m_prev - m_new)
        p = jnp.exp2(s_scr[hd] - m_new).astype(BF16)
        pv = _dot(vt_tile[kvh * VT_ROWS:(kvh + 1) * VT_ROWS, :], p)
        acc_scr[hd] = acc_scr[hd] * alpha + pv
        m_scr[hd] = m_new

    def step(cur, vt_tile, next_k, nxt, next_q=q_ref):
        if next_k is not None:
            for hd in range(SKEW):
                produce(nxt, next_k, hd, next_q)
        for hd in range(N_Q_HEADS):
            if next_k is not None and hd + SKEW < N_Q_HEADS:
                produce(nxt, next_k, hd + SKEW, next_q)
            consume(cur, hd, vt_tile)

    def k_at(j):
        return kl_ref[pl.ds(pl.multiple_of(j * TK, TK), TK), :]

    def v_at(j):
        return vl_ref[:, pl.ds(pl.multiple_of(j * TK, TK), TK)]

    @pl.when(i == 0)
    def _():
        for hd in range(N_Q_HEADS):
            produce("a", k_at(0), hd)

    @pl.when(i < n_lat)
    def _():
        def body(jj, carry):
            for u in range(UNROLL):
                j = UNROLL * jj + u
                step("ab"[u % 2], v_at(j), k_at(j + 1), "ab"[(u + 1) % 2])
            return carry
        lax.fori_loop(0, n_steps // UNROLL - 1, body, 0)
        for u in range(UNROLL - 1):
            j = n_steps - UNROLL + u
            step("ab"[u % 2], v_at(j), k_at(j + 1), "ab"[(u + 1) % 2])
        step("b", v_at(n_steps - 1), kc_ref[...], "c")

    @pl.when(i >= n_lat)
    def _():
        for hd in range(N_Q_HEADS):
            produce("c", kc_ref[...], hd)

    @pl.when(i < n_lat - 1)
    def _():
        step("c", vc_ref[...], kn_ref[...], "a", qn_ref)

    @pl.when(i >= n_lat - 1)
    def _():
        step("c", vc_ref[...], None, None)

    for slab in range(N_Q_HEADS // 2):
        halves = []
        for hd in (2 * slab, 2 * slab + 1):
            a = acc_scr[hd]
            halves.append(a[0:HEAD_DIM, :] / a[HEAD_DIM:HEAD_DIM + 1, :])
        out = jnp.concatenate(halves, axis=0).T
        o_ref[:, slab * LANES:(slab + 1) * LANES] = out.astype(o_ref.dtype)


def _attn(q, kk, vt, geo, n_tiles):
    n_lat, n_s, n_l, bsz = geo["n_lat"], geo["n_s"], geo["n_l"], geo["B"]
    s_len, l_len = n_s * TM, n_l * TM
    ctx_blk0 = (bsz * s_len) // l_len
    vt_rows = N_KV_HEADS * VT_ROWS

    def lat_b(i):
        return jnp.minimum(i // n_s, bsz - 1)

    def ctx_b(i):
        return ctx_blk0 + jnp.where(i < n_lat, i // n_s, (i - n_lat) // n_l)

    def nxt(i):
        return jnp.minimum(i + 1, n_tiles - 1)

    return pl.pallas_call(
        functools.partial(_attn_kernel, n_lat=n_lat, s_len=s_len),
        grid=(n_tiles,),
        in_specs=[
            pl.BlockSpec((TM, ATTN_WIDTH), lambda i: (i, 0)),
            pl.BlockSpec((TM, ATTN_WIDTH), lambda i: (nxt(i), 0)),
            pl.BlockSpec((s_len, 4 * LANES), lambda i: (lat_b(i), 0)),
            pl.BlockSpec((vt_rows, s_len), lambda i: (0, lat_b(i))),
            pl.BlockSpec((l_len, 4 * LANES), lambda i: (ctx_b(i), 0)),
            pl.BlockSpec((vt_rows, l_len), lambda i: (0, ctx_b(i))),
            pl.BlockSpec((TK, 4 * LANES), lambda i: (lat_b(nxt(i)) * (s_len // TK), 0)),
        ],
        out_specs=pl.BlockSpec((TM, ATTN_WIDTH), lambda i: (i, 0)),
        out_shape=jax.ShapeDtypeStruct((n_tiles * TM, ATTN_WIDTH), BF16),
        scratch_shapes=[pltpu.VMEM((N_Q_HEADS, 1, TM), F32),
                        pltpu.VMEM((N_Q_HEADS, VT_ROWS, TM), F32),
                        pltpu.VMEM((N_Q_HEADS, TK, TM), F32),
                        pltpu.VMEM((N_Q_HEADS, TK, TM), F32),
                        pltpu.VMEM((N_Q_HEADS, l_len, TM), F32),
                        pltpu.VMEM((N_Q_HEADS, 1, TM), F32),
                        pltpu.VMEM((N_Q_HEADS, 1, TM), F32),
                        pltpu.VMEM((N_Q_HEADS, 1, TM), F32)],
        compiler_params=_cparams(1),
        name="attn",
    )(q, q, kk, vt, kk, vt, kk)


def _merge_ffn_kernel(xa_ref, xb_ref, mod_ref, ab_ref, z_ref, zp_ref, zn_ref, at_ref, gu_ref, svn_ref,
                      gate_ref, cw_ref, ws_ref, bt_ref, wa_ref, wb_ref, wc_ref, wo_ref,
                      n2_ref, w1_ref, w3_ref, w2_ref, o_ref,
                      *, d, n_lat, n_s, n_l, ff_chunk, split):
    i = pl.program_id(0)
    tm = xa_ref.shape[0]
    t_in_seq = jnp.where(i < n_lat, i % n_s, (i - n_lat) % n_l)
    seq_tiles = jnp.where(i < n_lat, n_s, n_l)
    has_prev = (t_in_seq > 0).astype(F32)
    has_next = (t_in_seq < seq_tiles - 1).astype(F32)

    yb = _dot(at_ref[...], wb_ref[...])

    z = z_ref[...]
    row = lax.broadcasted_iota(jnp.int32, z.shape, 0)
    prev_row = zp_ref[SUBLANES - 1:SUBLANES, :] * has_prev
    next_row = zn_ref[0:1, :] * has_next
    zp = jnp.where(row == 0, prev_row, pltpu.roll(z, 1, axis=0))
    zn = jnp.where(row == tm - 1, next_row, pltpu.roll(z, tm - 1, axis=0))
    conv = zp * cw_ref[0:1, :] + z * cw_ref[1:2, :] + zn * cw_ref[2:3, :]
    ya = _dot((ab_ref[...] * conv).astype(BF16), wa_ref[...])

    sgw = svn_ref.shape[1]
    gw = sgw // SG_GROUPS
    lane = lax.broadcasted_iota(jnp.int32, (CHUNK, sgw), 1)
    parts = []
    for c in range(tm // CHUNK):
        vc = svn_ref[c * CHUNK:(c + 1) * CHUNK, :]
        stacked = jnp.concatenate(
            [jnp.where((lane >= g * gw) & (lane < (g + 1) * gw), vc, jnp.zeros_like(vc))
             for g in range(SG_GROUPS)], axis=0)
        mixed = _dot(ws_ref[...], stacked) + bt_ref[...]
        parts.append(gu_ref[c * CHUNK:(c + 1) * CHUNK, :] * mixed)
    yc = _dot(jnp.concatenate(parts, axis=0).astype(BF16), wc_ref[...])

    y = gate_ref[:, 0:d] * ya + gate_ref[:, d:2 * d] * yb + gate_ref[:, 2 * d:3 * d] * yc
    out = _dot(y.astype(BF16), wo_ref[...])
    x0 = xa_ref[...]
    if split:
        x0 = jnp.where(i < n_lat, x0, xb_ref[...])
    x1 = x0 + mod_ref[:, 2 * d:3 * d] * out

    shift = mod_ref[:, 3 * d:4 * d]
    scale = mod_ref[:, 4 * d:5 * d]
    hb = ((_rms(x1) * n2_ref[...]) * (1.0 + scale) + shift).astype(BF16)
    d_ff = w1_ref.shape[1]
    acc = None
    for c in range(d_ff // ff_chunk):
        cs = slice(c * ff_chunk, (c + 1) * ff_chunk)
        a = _dot(hb, w1_ref[:, cs])
        b = _dot(hb, w3_ref[:, cs])
        act = ((a * jax.nn.sigmoid(a)) * b).astype(BF16)
        part = _dot(act, w2_ref[cs, :])
        acc = part if acc is None else acc + part
    o_ref[...] = x1 + mod_ref[:, 5 * d:6 * d] * acc


def _merge_ffn(xa, xb, mods, l, ab, z, attn, gu, svn, gate, conv_w, w_s, bias_tab,
               w_a, w_b, w_c, w_o, norm2, w1, w3, w2, geo, n_tiles):
    split = xb is not None
    d = xa.shape[1]
    cw = ab.shape[1]
    sgw = gu.shape[1]
    d_ff = w1.shape[-1]
    ff_chunk = d_ff // 2 if (d_ff // 2) % LANES == 0 else d_ff
    n_lat, n_s, n_l, bsz = geo["n_lat"], geo["n_s"], geo["n_l"], geo["B"]
    rows_per_tile = TM // SUBLANES
    last_blk = z.shape[0] // SUBLANES - 1

    def mod_row(i):
        return jnp.where(i < n_lat, i // n_s, bsz)

    row = lambda w: pl.BlockSpec((TM, w), lambda i: (i, 0))
    stacked = lambda a, b: pl.BlockSpec((None, a, b), lambda i: (l, 0, 0),
                                        pipeline_mode=pl.Buffered(1))
    whole = lambda a, b: pl.BlockSpec((a, b), lambda i: (0, 0), pipeline_mode=pl.Buffered(1))
    return pl.pallas_call(
        functools.partial(_merge_ffn_kernel, d=d, n_lat=n_lat, n_s=n_s, n_l=n_l,
                          ff_chunk=ff_chunk, split=split),
        grid=(n_tiles,),
        in_specs=[
            pl.BlockSpec((TM, d), lambda i: (jnp.minimum(i, n_lat - 1) if split else i, 0)),
            pl.BlockSpec((TM, d), lambda i: (jnp.maximum(i - n_lat, 0) if split else 0, 0)),
            pl.BlockSpec((None, None, 1, N_MOD * d), lambda i: (l, mod_row(i), 0, 0)),
            row(cw),
            row(cw),
            pl.BlockSpec((SUBLANES, cw), lambda i: (jnp.maximum(i * rows_per_tile - 1, 0), 0)),
            pl.BlockSpec((SUBLANES, cw),
                         lambda i: (jnp.minimum((i + 1) * rows_per_tile, last_blk), 0)),
            row(ATTN_WIDTH),
            row(sgw),
            row(sgw),
            row(N_BRANCH * d),
            stacked(CONV_K, cw),
            stacked(CHUNK, SG_GROUPS * CHUNK),
            stacked(CHUNK, sgw),
            whole(cw, d),
            whole(ATTN_WIDTH, d),
            whole(sgw, d),
            whole(d, d),
            stacked(1, d),
            whole(d, d_ff),
            whole(d, d_ff),
            whole(d_ff, d),
        ],
        out_specs=row(d),
        out_shape=jax.ShapeDtypeStruct((n_tiles * TM, d), F32),
        compiler_params=_cparams(1),
        name="merge_ffn",
    )(xa, xb if split else xa, mods, ab, z, z, z, attn, gu, svn, gate, conv_w, w_s, bias_tab,
      w_a, w_b, w_c, w_o, norm2, w1, w3, w2)


def _rope_table(s_len, pad):
    t = jnp.arange(s_len, dtype=jnp.int32)
    pos = jnp.stack([(t // GRID_W).astype(F32), (t % GRID_W).astype(F32)], axis=1)
    inv_freq = ROPE_THETA ** (-jnp.arange(0, AXIS_DIM, 2, dtype=F32) / AXIS_DIM)
    ang = pos[:, :, None] * inv_freq
    cos, sin = jnp.cos(ang), jnp.sin(ang)
    zero = jnp.zeros_like(sin)
    c64 = jnp.stack([cos, cos], axis=2).reshape(s_len, HEAD_DIM)
    a64 = jnp.stack([-sin, zero], axis=2).reshape(s_len, HEAD_DIM)
    b64 = jnp.stack([zero, sin], axis=2).reshape(s_len, HEAD_DIM)
    tab = jnp.stack([c64, a64, b64], axis=0)
    ident = jnp.stack([jnp.ones((pad, HEAD_DIM), F32), jnp.zeros((pad, HEAD_DIM), F32),
                       jnp.zeros((pad, HEAD_DIM), F32)], axis=0)
    tab = jnp.concatenate([tab, ident], axis=1)
    return jnp.tile(tab, (1, 1, LANES // HEAD_DIM))


def _head_mean_matrix():
    r = jnp.arange(MXU_DIM) // HEAD_DIM
    return jnp.where(r[:, None] == r[None, :], 1.0 / HEAD_DIM, 0.0).astype(BF16)


def kernel(x, c, ctx, c_ctx, w_mod, b_mod, norm1, w_in, q_gain, k_gain, conv_w, sg_norm,
           w_s, b_s, w_a, w_b, w_c, w_o, norm2, w_ff1, w_ff3, w_ff2):
    bsz, s_len, d = x.shape
    l_len = ctx.shape[1]
    depth = w_mod.shape[0]
    sgw = sg_norm.shape[1]
    assert s_len % TM == 0 and l_len % TM == 0 and s_len % (UNROLL * TK) == 0 and s_len % GRID_W == 0
    assert (bsz * s_len) % l_len == 0 and bsz + 1 <= MOD_ROWS
    assert s_len % TP == 0 and (bsz * l_len) % TP == 0 and TP % TM == 0
    n_s, n_l = s_len // TM, l_len // TM
    geo = {"B": bsz, "n_s": n_s, "n_l": n_l, "n_lat": bsz * n_s}
    n_all = bsz * (n_s + n_l)

    cond = jnp.concatenate(
        [c, c_ctx[None, :], jnp.zeros((MOD_ROWS - bsz - 1, d), F32)], axis=0)
    mods = _adaln_all(cond, w_mod, b_mod).reshape(depth, MOD_ROWS, 1, N_MOD * d)

    rope = _rope_table(s_len, TP)
    e = _head_mean_matrix()
    qg = jnp.tile(q_gain, (1, N_Q_HEADS)).reshape(depth, 1, ATTN_WIDTH)
    kg = jnp.tile(k_gain, (1, N_KV_HEADS)).reshape(depth, 1, KV_WIDTH)
    sgn = sg_norm.reshape(depth, 1, sgw)
    n1 = norm1.reshape(depth, 1, d)
    n2 = norm2.reshape(depth, 1, d)
    bias_tab = jnp.repeat(jnp.swapaxes(b_s, 1, 2), sgw // SG_GROUPS, axis=2)
    w_s_b = jnp.transpose(w_s, (0, 2, 1, 3)).reshape(depth, CHUNK, SG_GROUPS * CHUNK).astype(BF16)
    later = [w_a, w_b, w_c, w_o, w_ff1, w_ff3, w_ff2]
    w_in_l = w_in[0].astype(BF16)

    xa, xb = x.reshape(bsz * s_len, d), ctx.reshape(bsz * l_len, d)
    for l in range(depth):
        last = l == depth - 1
        n_tiles = geo["n_lat"] if last else n_all
        riders = [(w, l) for w in later] + ([] if last else [(w_in, l + 1)])
        (ab, z, q, kk, vt, gu, svn, gate), cast = _proj(
            xa, xb, mods, l, n1, w_in_l, qg, kg, sgn, rope, e, geo, riders)
        wa_l, wb_l, wc_l, wo_l, w1_l, w3_l, w2_l = cast[:len(later)]
        if not last:
            w_in_l = cast[len(later)]
        attn = _attn(q, kk, vt, geo, n_tiles)
        xa = _merge_ffn(xa, xb, mods, l, ab, z, attn, gu, svn, gate, conv_w, w_s_b,
                        bias_tab, wa_l, wb_l, wc_l, wo_l, n2, w1_l, w3_l, w2_l, geo, n_tiles)
        xb = None
    return xa.reshape(bsz, s_len, d)
```

```python
import functools
import math

import jax
import jax.numpy as jnp
from jax import lax
from jax.experimental import pallas as pl
from jax.experimental.pallas import tpu as pltpu

F32 = jnp.float32
BF16 = jnp.bfloat16

GRID_W = 64
HEAD_DIM = 64
N_Q_HEADS = 8
N_KV_HEADS = 2
GROUP = N_Q_HEADS // N_KV_HEADS
ATTN_WIDTH = N_Q_HEADS * HEAD_DIM
KV_WIDTH = N_KV_HEADS * HEAD_DIM
AXIS_DIM = HEAD_DIM // 2
ROPE_THETA = 10000.0
CONV_K = 3
CHUNK = 128
SG_GROUPS = 4
N_BRANCH = 3
N_MOD = 6
EPS = 1e-6

LANES = 128
MXU_DIM = 256
SUBLANES = 8
TM = 256
TP = 512
TF = 512
FF_ROWS = 256
TK = 512
ONES_ROWS = 16
VT_ROWS = HEAD_DIM + ONES_ROWS
Q_SCALE = HEAD_DIM ** -0.5 * math.log2(math.e)
SKEW = 2
UNROLL = 4
MOD_ROWS = 8
N_PROJ_IN = 10
N_PROJ_OUT = 8
VMEM_LIMIT = 56 * 1024 * 1024


def _cparams(n_axes=1):
    return pltpu.CompilerParams(dimension_semantics=("arbitrary",) * n_axes,
                                vmem_limit_bytes=VMEM_LIMIT)


def _dot(a, b):
    return jnp.dot(a, b, preferred_element_type=F32)


def _dot_t(a, b):
    return lax.dot_general(a, b, (((1,), (1,)), ((), ())), preferred_element_type=F32)


def _mod_kernel(cond_ref, w_ref, b_ref, o_ref):
    cnd = cond_ref[...]
    s = (cnd * jax.nn.sigmoid(cnd)).astype(BF16)
    o_ref[...] = _dot(s, w_ref[...].astype(BF16)) + b_ref[...]


def _adaln_all(cond, w_mod, b_mod):
    depth, d, width = w_mod.shape
    tn = width // 4
    return pl.pallas_call(
        _mod_kernel,
        grid=(depth, width // tn),
        in_specs=[
            pl.BlockSpec((MOD_ROWS, d), lambda l, j: (0, 0)),
            pl.BlockSpec((None, d, tn), lambda l, j: (l, 0, j)),
            pl.BlockSpec((None, 1, tn), lambda l, j: (l, 0, j)),
        ],
        out_specs=pl.BlockSpec((None, MOD_ROWS, tn), lambda l, j: (l, 0, j)),
        out_shape=jax.ShapeDtypeStruct((depth, MOD_ROWS, width), F32),
        compiler_params=_cparams(2),
        name="adaln",
    )(cond, w_mod, b_mod.reshape(depth, 1, width))


def _rms(x, eps=EPS):
    return x * lax.rsqrt(jnp.mean(x * x, axis=-1, keepdims=True) + eps)


def _gelu_tanh(x):
    c = math.sqrt(2.0 / math.pi)
    return 0.5 * x * (1.0 + jnp.tanh(c * (x + 0.044715 * (x * x * x))))


def _head_mean_sq(p, e):
    sq = p * p
    hi = sq.astype(BF16)
    lo = (sq - hi.astype(F32)).astype(BF16)
    blk = min(p.shape[1], e.shape[0])
    eb = e[0:blk, 0:blk]
    cols = [_dot(hi[:, c:c + blk], eb) + _dot(lo[:, c:c + blk], eb)
            for c in range(0, p.shape[1], blk)]
    return cols[0] if len(cols) == 1 else jnp.concatenate(cols, axis=1)


def _rope_slab(t, cos, sa, sb):
    up = pltpu.roll(t, LANES - 16, axis=1)
    dn = pltpu.roll(t, 16, axis=1)
    return t * cos + up * sa + dn * sb


def _proj_kernel(*refs, d, cw, sgw, n_lat, n_rid, split):
    (xa_ref, xb_ref, mod_ref, n1_ref, w_ref, qg_ref, kg_ref, sgn_ref, rope_ref,
     e_ref) = refs[:N_PROJ_IN]
    rid_in = refs[N_PROJ_IN:N_PROJ_IN + n_rid]
    (ab_ref, z_ref, q_ref, kk_ref, vt_ref, gu_ref, svn_ref,
     gate_ref) = refs[N_PROJ_IN + n_rid:N_PROJ_IN + n_rid + N_PROJ_OUT]
    rid_out = refs[N_PROJ_IN + n_rid + N_PROJ_OUT:]
    off_q = 3 * cw
    off_k = off_q + ATTN_WIDTH
    off_v = off_k + KV_WIDTH
    off_u = off_v + KV_WIDTH
    off_g = off_u + 2 * sgw

    for src, dst in zip(rid_in, rid_out):
        dst[...] = src[...].astype(BF16)

    x = xa_ref[...]
    if split:
        x = jnp.where(pl.program_id(0) < n_lat, x, xb_ref[...])
    shift = mod_ref[:, 0:d]
    scale = mod_ref[:, d:2 * d]
    h = (_rms(x) * n1_ref[...]) * (1.0 + scale) + shift
    hb = h.astype(BF16)

    cos = rope_ref[0]
    sa = rope_ref[1]
    sb = rope_ref[2]
    e = e_ref[...]

    pq = _dot(hb, w_ref[:, off_q:off_k])
    qn = pq * lax.rsqrt(_head_mean_sq(pq, e) + EPS) * qg_ref[...]
    for s in range(ATTN_WIDTH // LANES):
        r = _rope_slab(qn[:, s * LANES:(s + 1) * LANES], cos, sa, sb)
        q_ref[:, s * LANES:(s + 1) * LANES] = (r * Q_SCALE).astype(BF16)

    pk = _dot(hb, w_ref[:, off_k:off_v])
    kn = pk * lax.rsqrt(_head_mean_sq(pk, e) + EPS) * kg_ref[...]
    kr = _rope_slab(kn, cos, sa, sb)
    ksw = pltpu.roll(kr, HEAD_DIM, axis=1)
    lane = lax.broadcasted_iota(jnp.int32, kr.shape, 1)
    lo_half = lane < HEAD_DIM
    zero = jnp.zeros_like(kr)
    kk_ref[:, 0 * LANES:1 * LANES] = jnp.where(lo_half, kr, zero).astype(BF16)
    kk_ref[:, 1 * LANES:2 * LANES] = jnp.where(lo_half, zero, ksw).astype(BF16)
    kk_ref[:, 2 * LANES:3 * LANES] = jnp.where(lo_half, ksw, zero).astype(BF16)
    kk_ref[:, 3 * LANES:4 * LANES] = jnp.where(lo_half, zero, kr).astype(BF16)

    pvt = _dot(hb, w_ref[:, off_v:off_u]).T.astype(BF16)
    ones = jnp.ones((ONES_ROWS, pvt.shape[1]), BF16)
    for hh in range(N_KV_HEADS):
        vt_ref[hh * VT_ROWS:hh * VT_ROWS + HEAD_DIM, :] = pvt[hh * HEAD_DIM:(hh + 1) * HEAD_DIM, :]
        vt_ref[hh * VT_ROWS + HEAD_DIM:(hh + 1) * VT_ROWS, :] = ones

    pu = _dot(hb, w_ref[:, off_u:off_g])
    gu_ref[...] = _gelu_tanh(pu[:, 0:sgw])
    svn_ref[...] = (_rms(_gelu_tanh(pu[:, sgw:2 * sgw])) * sgn_ref[...]).astype(BF16)

    for j in range(N_BRANCH):
        pg = _dot(hb, w_ref[:, off_g + j * d:off_g + (j + 1) * d])
        gate_ref[:, j * d:(j + 1) * d] = jax.nn.sigmoid(pg)

    pc = _dot(hb, w_ref[:, 0:off_q])
    ab_ref[...] = pc[:, 0:cw]
    z_ref[...] = pc[:, cw:2 * cw] * pc[:, 2 * cw:3 * cw]


def _proj(xa, xb, mods, l, norm1, w_in, qg, kg, sgn, rope, e, geo, riders):
    split = xb is not None
    d = xa.shape[1]
    in_w = w_in.shape[-1]
    cw = d // 4
    sgw = d // 4
    bsz = geo["B"]
    n_s = geo["n_s"] * TM // TP
    n_lat = bsz * n_s
    n_t = n_lat + bsz * geo["n_l"] * TM // TP
    ntok = n_t * TP

    def rider_blocks(rows):
        nb = 1
        while 2 * nb <= n_t and rows % (2 * nb * 2 * SUBLANES) == 0:
            nb *= 2
        return nb

    n_blks = [rider_blocks(a.shape[1]) for a, _ in riders]

    def mod_row(i):
        return jnp.where(i < n_lat, i // n_s, bsz)

    def rope_blk(i):
        return jnp.where(i < n_lat, i % n_s, n_s)


    row = lambda w: pl.BlockSpec((TP, w), lambda i: (i, 0))
    const = lambda *blk: pl.BlockSpec((None,) + blk, lambda i: (l,) + (0,) * len(blk),
                                      pipeline_mode=pl.Buffered(1))
    outs = [(cw, F32), (cw, F32), (ATTN_WIDTH, BF16), (4 * LANES, BF16), None,
            (sgw, F32), (sgw, BF16), (N_BRANCH * d, F32)]
    vt_rows = N_KV_HEADS * VT_ROWS
    out_specs = [pl.BlockSpec((vt_rows, TP), lambda i: (0, i)) if o is None else row(o[0])
                 for o in outs]
    out_shape = [jax.ShapeDtypeStruct((vt_rows, ntok), BF16) if o is None
                 else jax.ShapeDtypeStruct((ntok, o[0]), o[1]) for o in outs]
    rid_in_specs = [pl.BlockSpec((None, a.shape[1] // nb, a.shape[2]),
                                 lambda i, li=li, nb=nb: (li, jnp.minimum(i, nb - 1), 0))
                    for (a, li), nb in zip(riders, n_blks)]
    out_specs += [pl.BlockSpec((a.shape[1] // nb, a.shape[2]),
                               lambda i, nb=nb: (jnp.minimum(i, nb - 1), 0))
                  for (a, _), nb in zip(riders, n_blks)]
    out_shape += [jax.ShapeDtypeStruct(a.shape[1:], BF16) for a, _ in riders]
    res = pl.pallas_call(
        functools.partial(_proj_kernel, d=d, cw=cw, sgw=sgw, n_lat=n_lat, n_rid=len(riders),
                          split=split),
        grid=(n_t,),
        in_specs=[
            pl.BlockSpec((TP, d), lambda i: (jnp.minimum(i, n_lat - 1) if split else i, 0)),
            pl.BlockSpec((TP, d), lambda i: (jnp.maximum(i - n_lat, 0) if split else 0, 0)),
            pl.BlockSpec((None, None, 1, N_MOD * d), lambda i: (l, mod_row(i), 0, 0)),
            const(1, d),
            pl.BlockSpec((d, in_w), lambda i: (0, 0), pipeline_mode=pl.Buffered(1)),
            const(1, ATTN_WIDTH),
            const(1, KV_WIDTH),
            const(1, sgw),
            pl.BlockSpec((3, TP, LANES), lambda i: (0, rope_blk(i), 0)),
            pl.BlockSpec(e.shape, lambda i: (0, 0), pipeline_mode=pl.Buffered(1)),
        ] + rid_in_specs,
        out_specs=out_specs,
        out_shape=out_shape,
        compiler_params=_cparams(1),
        name="proj",
    )(xa, xb if split else xa, mods, norm1, w_in, qg, kg, sgn, rope, e, *[a for a, _ in riders])
    return res[:N_PROJ_OUT], res[N_PROJ_OUT:]


def _attn_kernel(q_ref, qn_ref, kl_ref, vl_ref, kc_ref, vc_ref, kn_ref, o_ref,
                 m_scr, acc_scr, sa_scr, sb_scr, sc_scr, ma_scr, mb_scr, mc_scr, *, n_lat, s_len):
    i = pl.program_id(0)
    n_steps = s_len // TK
    m_scr[...] = jnp.full(m_scr.shape, -jnp.inf, F32)
    acc_scr[...] = jnp.zeros(acc_scr.shape, F32)

    bufs = {"a": (sa_scr, ma_scr), "b": (sb_scr, mb_scr), "c": (sc_scr, mc_scr)}

    def produce(buf, k_tile, hd, qr=q_ref):
        slab, par, kvh = hd // 2, hd % 2, hd // GROUP
        qs = qr[:, slab * LANES:(slab + 1) * LANES]
        kcol = (2 * kvh + par) * LANES
        s = _dot_t(k_tile[:, kcol:kcol + LANES], qs)
        s_scr, mx_scr = bufs[buf]
        s_scr[hd] = s
        mx_scr[hd] = jnp.max(s, axis=0, keepdims=True)

    def consume(buf, hd, vt_tile):
        s_scr, mx_scr = bufs[buf]
        kvh = hd // GROUP
        m_prev = m_scr[hd]
        m_new = jnp.maximum(m_prev, mx_scr[hd])
        alpha = jnp.exp2(m_prev - m_new)
        p = jnp.exp2(s_scr[hd] - m_new).astype(BF16)
        pv = _dot(vt_tile[kvh * VT_ROWS:(kvh + 1) * VT_ROWS, :], p)
        acc_scr[hd] = acc_scr[hd] * alpha + pv
        m_scr[hd] = m_new

    def step(cur, vt_tile, next_k, nxt, next_q=q_ref):
        if next_k is not None:
            for hd in range(SKEW):
                produce(nxt, next_k, hd, next_q)
        for hd in range(N_Q_HEADS):
            if next_k is not None and hd + SKEW < N_Q_HEADS:
                produce(nxt, next_k, hd + SKEW, next_q)
            consume(cur, hd, vt_tile)

    def k_at(j):
        return kl_ref[pl.ds(pl.multiple_of(j * TK, TK), TK), :]

    def v_at(j):
        return vl_ref[:, pl.ds(pl.multiple_of(j * TK, TK), TK)]

    @pl.when(i == 0)
    def _():
        for hd in range(N_Q_HEADS):
            produce("a", k_at(0), hd)

    @pl.when(i < n_lat)
    def _():
        def body(jj, carry):
            for u in range(UNROLL):
                j = UNROLL * jj + u
                step("ab"[u % 2], v_at(j), k_at(j + 1), "ab"[(u + 1) % 2])
            return carry
        lax.fori_loop(0, n_steps // UNROLL - 1, body, 0)
        for u in range(UNROLL - 1):
            j = n_steps - UNROLL + u
            step("ab"[u % 2], v_at(j), k_at(j + 1), "ab"[(u + 1) % 2])
        step("b", v_at(n_steps - 1), kc_ref[...], "c")

    @pl.when(i >= n_lat)
    def _():
        for hd in range(N_Q_HEADS):
            produce("c", kc_ref[...], hd)

    @pl.when(i < n_lat - 1)
    def _():
        step("c", vc_ref[...], kn_ref[...], "a", qn_ref)

    @pl.when(i >= n_lat - 1)
    def _():
        step("c", vc_ref[...], None, None)

    for slab in range(N_Q_HEADS // 2):
        halves = []
        for hd in (2 * slab, 2 * slab + 1):
            a = acc_scr[hd]
            halves.append(a[0:HEAD_DIM, :] / a[HEAD_DIM:HEAD_DIM + 1, :])
        out = jnp.concatenate(halves, axis=0).T
        o_ref[:, slab * LANES:(slab + 1) * LANES] = out.astype(o_ref.dtype)


def _attn(q, kk, vt, geo, n_tiles):
    n_lat, n_s, n_l, bsz = geo["n_lat"], geo["n_s"], geo["n_l"], geo["B"]
    s_len, l_len = n_s * TM, n_l * TM
    ctx_blk0 = (bsz * s_len) // l_len
    vt_rows = N_KV_HEADS * VT_ROWS

    def lat_b(i):
        return jnp.minimum(i // n_s, bsz - 1)

    def ctx_b(i):
        return ctx_blk0 + jnp.where(i < n_lat, i // n_s, (i - n_lat) // n_l)

    def nxt(i):
        return jnp.minimum(i + 1, n_tiles - 1)

    return pl.pallas_call(
        functools.partial(_attn_kernel, n_lat=n_lat, s_len=s_len),
        grid=(n_tiles,),
        in_specs=[
            pl.BlockSpec((TM, ATTN_WIDTH), lambda i: (i, 0)),
            pl.BlockSpec((TM, ATTN_WIDTH), lambda i: (nxt(i), 0)),
            pl.BlockSpec((s_len, 4 * LANES), lambda i: (lat_b(i), 0)),
            pl.BlockSpec((vt_rows, s_len), lambda i: (0, lat_b(i))),
            pl.BlockSpec((l_len, 4 * LANES), lambda i: (ctx_b(i), 0)),
            pl.BlockSpec((vt_rows, l_len), lambda i: (0, ctx_b(i))),
            pl.BlockSpec((TK, 4 * LANES), lambda i: (lat_b(nxt(i)) * (s_len // TK), 0)),
        ],
        out_specs=pl.BlockSpec((TM, ATTN_WIDTH), lambda i: (i, 0)),
        out_shape=jax.ShapeDtypeStruct((n_tiles * TM, ATTN_WIDTH), BF16),
        scratch_shapes=[pltpu.VMEM((N_Q_HEADS, 1, TM), F32),
                        pltpu.VMEM((N_Q_HEADS, VT_ROWS, TM), F32),
                        pltpu.VMEM((N_Q_HEADS, TK, TM), F32),
                        pltpu.VMEM((N_Q_HEADS, TK, TM), F32),
                        pltpu.VMEM((N_Q_HEADS, l_len, TM), F32),
                        pltpu.VMEM((N_Q_HEADS, 1, TM), F32),
                        pltpu.VMEM((N_Q_HEADS, 1, TM), F32),
                        pltpu.VMEM((N_Q_HEADS, 1, TM), F32)],
        compiler_params=_cparams(1),
        name="attn",
    )(q, q, kk, vt, kk, vt, kk)


def _merge_ffn_kernel(xa_ref, xb_ref, mod_ref, ab_ref, z_ref, zp_ref, zn_ref, at_ref, gu_ref, svn_ref,
                      gate_ref, cw_ref, ws_ref, bt_ref, wa_ref, wb_ref, wc_ref, wo_ref,
                      n2_ref, w1_ref, w3_ref, w2_ref, o_ref,
                      *, d, n_lat, n_s, l_rows, ff_chunk, split):
    i = pl.program_id(0)
    tm = xa_ref.shape[0]
    is_lat = i < n_lat
    t_in_seq = i % n_s

    yb = _dot(at_ref[...], wb_ref[...])

    z = z_ref[...]
    row = lax.broadcasted_iota(jnp.int32, z.shape, 0)
    first = row == jnp.where(is_lat & (t_in_seq == 0), 0, -1)
    last = row == jnp.where(is_lat & (t_in_seq == n_s - 1), tm - 1, -1)
    for k in range(tm // l_rows):
        first = first | (row == jnp.where(is_lat, -1, k * l_rows))
        last = last | (row == jnp.where(is_lat, -1, (k + 1) * l_rows - 1))
    zp = jnp.where(row == 0, zp_ref[SUBLANES - 1:SUBLANES, :], pltpu.roll(z, 1, axis=0))
    zn = jnp.where(row == tm - 1, zn_ref[0:1, :], pltpu.roll(z, tm - 1, axis=0))
    zp = jnp.where(first, jnp.zeros_like(z), zp)
    zn = jnp.where(last, jnp.zeros_like(z), zn)
    conv = zp * cw_ref[0:1, :] + z * cw_ref[1:2, :] + zn * cw_ref[2:3, :]
    ya = _dot((ab_ref[...] * conv).astype(BF16), wa_ref[...])

    sgw = svn_ref.shape[1]
    gw = sgw // SG_GROUPS
    lane = lax.broadcasted_iota(jnp.int32, (CHUNK, sgw), 1)
    parts = []
    for c in range(tm // CHUNK):
        vc = svn_ref[c * CHUNK:(c + 1) * CHUNK, :]
        stacked = jnp.concatenate(
            [jnp.where((lane >= g * gw) & (lane < (g + 1) * gw), vc, jnp.zeros_like(vc))
             for g in range(SG_GROUPS)], axis=0)
        mixed = _dot(ws_ref[...], stacked) + bt_ref[...]
        parts.append(gu_ref[c * CHUNK:(c + 1) * CHUNK, :] * mixed)
    yc = _dot(jnp.concatenate(parts, axis=0).astype(BF16), wc_ref[...])

    y = gate_ref[:, 0:d] * ya + gate_ref[:, d:2 * d] * yb + gate_ref[:, 2 * d:3 * d] * yc
    out = _dot(y.astype(BF16), wo_ref[...])
    x0 = xa_ref[...]
    if split:
        x0 = jnp.where(i < n_lat, x0, xb_ref[...])
    x1 = x0 + mod_ref[:, 2 * d:3 * d] * out

    shift = mod_ref[:, 3 * d:4 * d]
    scale = mod_ref[:, 4 * d:5 * d]
    hb = ((_rms(x1) * n2_ref[...]) * (1.0 + scale) + shift).astype(BF16)
    d_ff = w1_ref.shape[1]
    for r in range(0, tm, FF_ROWS):
        rs = slice(r, r + FF_ROWS)
        acc = None
        for c in range(d_ff // ff_chunk):
            cs = slice(c * ff_chunk, (c + 1) * ff_chunk)
            a = _dot(hb[rs], w1_ref[:, cs])
            b = _dot(hb[rs], w3_ref[:, cs])
            act = ((a * jax.nn.sigmoid(a)) * b).astype(BF16)
            part = _dot(act, w2_ref[cs, :])
            acc = part if acc is None else acc + part
        o_ref[rs, :] = x1[rs] + mod_ref[:, 5 * d:6 * d] * acc


def _merge_ffn(xa, xb, mods, l, ab, z, attn, gu, svn, gate, conv_w, w_s, bias_tab,
               w_a, w_b, w_c, w_o, norm2, w1, w3, w2, geo, n_tiles):
    split = xb is not None
    d = xa.shape[1]
    cw = ab.shape[1]
    sgw = gu.shape[1]
    d_ff = w1.shape[-1]
    ff_chunk = d_ff // 2 if (d_ff // 2) % LANES == 0 else d_ff
    bsz = geo["B"]
    n_s = geo["n_s"] * TM // TF
    n_lat = bsz * n_s
    n_tiles = n_tiles * TM // TF
    l_rows = geo["n_l"] * TM
    rows_per_tile = TF // SUBLANES
    last_blk = z.shape[0] // SUBLANES - 1

    def mod_row(i):
        return jnp.where(i < n_lat, i // n_s, bsz)

    row = lambda w: pl.BlockSpec((TF, w), lambda i: (i, 0))
    stacked = lambda a, b: pl.BlockSpec((None, a, b), lambda i: (l, 0, 0),
                                        pipeline_mode=pl.Buffered(1))
    whole = lambda a, b: pl.BlockSpec((a, b), lambda i: (0, 0), pipeline_mode=pl.Buffered(1))
    return pl.pallas_call(
        functools.partial(_merge_ffn_kernel, d=d, n_lat=n_lat, n_s=n_s, l_rows=l_rows,
                          ff_chunk=ff_chunk, split=split),
        grid=(n_tiles,),
        in_specs=[
            pl.BlockSpec((TF, d), lambda i: (jnp.minimum(i, n_lat - 1) if split else i, 0)),
            pl.BlockSpec((TF, d), lambda i: (jnp.maximum(i - n_lat, 0) if split else 0, 0)),
            pl.BlockSpec((None, None, 1, N_MOD * d), lambda i: (l, mod_row(i), 0, 0)),
            row(cw),
            row(cw),
            pl.BlockSpec((SUBLANES, cw), lambda i: (jnp.maximum(i * rows_per_tile - 1, 0), 0)),
            pl.BlockSpec((SUBLANES, cw),
                         lambda i: (jnp.minimum((i + 1) * rows_per_tile, last_blk), 0)),
            row(ATTN_WIDTH),
            row(sgw),
            row(sgw),
            row(N_BRANCH * d),
            stacked(CONV_K, cw),
            stacked(CHUNK, SG_GROUPS * CHUNK),
            stacked(CHUNK, sgw),
            whole(cw, d),
            whole(ATTN_WIDTH, d),
            whole(sgw, d),
            whole(d, d),
            stacked(1, d),
            whole(d, d_ff),
            whole(d, d_ff),
            whole(d_ff, d),
        ],
        out_specs=row(d),
        out_shape=jax.ShapeDtypeStruct((n_tiles * TF, d), F32),
        compiler_params=_cparams(1),
        name="merge_ffn",
    )(xa, xb if split else xa, mods, ab, z, z, z, attn, gu, svn, gate, conv_w, w_s, bias_tab,
      w_a, w_b, w_c, w_o, norm2, w1, w3, w2)


def _rope_table(s_len, pad):
    t = jnp.arange(s_len, dtype=jnp.int32)
    pos = jnp.stack([(t // GRID_W).astype(F32), (t % GRID_W).astype(F32)], axis=1)
    inv_freq = ROPE_THETA ** (-jnp.arange(0, AXIS_DIM, 2, dtype=F32) / AXIS_DIM)
    ang = pos[:, :, None] * inv_freq
    cos, sin = jnp.cos(ang), jnp.sin(ang)
    zero = jnp.zeros_like(sin)
    c64 = jnp.stack([cos, cos], axis=2).reshape(s_len, HEAD_DIM)
    a64 = jnp.stack([-sin, zero], axis=2).reshape(s_len, HEAD_DIM)
    b64 = jnp.stack([zero, sin], axis=2).reshape(s_len, HEAD_DIM)
    tab = jnp.stack([c64, a64, b64], axis=0)
    ident = jnp.stack([jnp.ones((pad, HEAD_DIM), F32), jnp.zeros((pad, HEAD_DIM), F32),
                       jnp.zeros((pad, HEAD_DIM), F32)], axis=0)
    tab = jnp.concatenate([tab, ident], axis=1)
    return jnp.tile(tab, (1, 1, LANES // HEAD_DIM))


def _head_mean_matrix():
    r = jnp.arange(MXU_DIM) // HEAD_DIM
    return jnp.where(r[:, None] == r[None, :], 1.0 / HEAD_DIM, 0.0).astype(BF16)


def kernel(x, c, ctx, c_ctx, w_mod, b_mod, norm1, w_in, q_gain, k_gain, conv_w, sg_norm,
           w_s, b_s, w_a, w_b, w_c, w_o, norm2, w_ff1, w_ff3, w_ff2):
    bsz, s_len, d = x.shape
    l_len = ctx.shape[1]
    depth = w_mod.shape[0]
    sgw = sg_norm.shape[1]
    assert s_len % TM == 0 and l_len % TM == 0 and s_len % (UNROLL * TK) == 0 and s_len % GRID_W == 0
    assert (bsz * s_len) % l_len == 0 and bsz + 1 <= MOD_ROWS
    assert s_len % TP == 0 and (bsz * l_len) % TP == 0 and TP % TM == 0
    assert s_len % TF == 0 and (bsz * l_len) % TF == 0 and TF % TM == 0 and TF % l_len == 0
    n_s, n_l = s_len // TM, l_len // TM
    geo = {"B": bsz, "n_s": n_s, "n_l": n_l, "n_lat": bsz * n_s}
    n_all = bsz * (n_s + n_l)

    cond = jnp.concatenate(
        [c, c_ctx[None, :], jnp.zeros((MOD_ROWS - bsz - 1, d), F32)], axis=0)
    mods = _adaln_all(cond, w_mod, b_mod).reshape(depth, MOD_ROWS, 1, N_MOD * d)

    rope = _rope_table(s_len, TP)
    e = _head_mean_matrix()
    qg = jnp.tile(q_gain, (1, N_Q_HEADS)).reshape(depth, 1, ATTN_WIDTH)
    kg = jnp.tile(k_gain, (1, N_KV_HEADS)).reshape(depth, 1, KV_WIDTH)
    sgn = sg_norm.reshape(depth, 1, sgw)
    n1 = norm1.reshape(depth, 1, d)
    n2 = norm2.reshape(depth, 1, d)
    bias_tab = jnp.repeat(jnp.swapaxes(b_s, 1, 2), sgw // SG_GROUPS, axis=2)
    w_s_b = jnp.transpose(w_s, (0, 2, 1, 3)).reshape(depth, CHUNK, SG_GROUPS * CHUNK).astype(BF16)
    later = [w_a, w_b, w_c, w_o, w_ff1, w_ff3, w_ff2]
    w_in_l = w_in[0].astype(BF16)

    xa, xb = x.reshape(bsz * s_len, d), ctx.reshape(bsz * l_len, d)
    for l in range(depth):
        last = l == depth - 1
        n_tiles = geo["n_lat"] if last else n_all
        riders = [(w, l) for w in later] + ([] if last else [(w_in, l + 1)])
        (ab, z, q, kk, vt, gu, svn, gate), cast = _proj(
            xa, xb, mods, l, n1, w_in_l, qg, kg, sgn, rope, e, geo, riders)
        wa_l, wb_l, wc_l, wo_l, w1_l, w3_l, w2_l = cast[:len(later)]
        if not last:
            w_in_l = cast[len(later)]
        attn = _attn(q, kk, vt, geo, n_tiles)
        xa = _merge_ffn(xa, xb, mods, l, ab, z, attn, gu, svn, gate, conv_w, w_s_b,
                        bias_tab, wa_l, wb_l, wc_l, wo_l, n2, w1_l, w3_l, w2_l, geo, n_tiles)
        xb = None
    return xa.reshape(bsz, s_len, d)
```

```python
import functools
import math

import jax
import jax.numpy as jnp
from jax import lax
from jax.experimental import pallas as pl
from jax.experimental.pallas import tpu as pltpu

F32 = jnp.float32
BF16 = jnp.bfloat16

GRID_W = 64
HEAD_DIM = 64
N_Q_HEADS = 8
N_KV_HEADS = 2
GROUP = N_Q_HEADS // N_KV_HEADS
ATTN_WIDTH = N_Q_HEADS * HEAD_DIM
KV_WIDTH = N_KV_HEADS * HEAD_DIM
AXIS_DIM = HEAD_DIM // 2
ROPE_THETA = 10000.0
CONV_K = 3
CHUNK = 128
SG_GROUPS = 4
N_BRANCH = 3
N_MOD = 6
EPS = 1e-6

LANES = 128
MXU_DIM = 256
SUBLANES = 8
TM = 256
TP = 512
TF = 512
FF_ROWS = 512
TK = 512
ONES_ROWS = 16
VT_ROWS = HEAD_DIM + ONES_ROWS
Q_SCALE = HEAD_DIM ** -0.5 * math.log2(math.e)
SKEW = 2
UNROLL = 4
MOD_ROWS = 8
N_PROJ_IN = 10
N_PROJ_OUT = 8
VMEM_LIMIT = 56 * 1024 * 1024


def _cparams(n_axes=1):
    return pltpu.CompilerParams(dimension_semantics=("arbitrary",) * n_axes,
                                vmem_limit_bytes=VMEM_LIMIT)


def _dot(a, b):
    return jnp.dot(a, b, preferred_element_type=F32)


def _dot_t(a, b):
    return lax.dot_general(a, b, (((1,), (1,)), ((), ())), preferred_element_type=F32)


def _mod_kernel(cond_ref, w_ref, b_ref, o_ref):
    cnd = cond_ref[...]
    s = (cnd * jax.nn.sigmoid(cnd)).astype(BF16)
    o_ref[...] = _dot(s, w_ref[...].astype(BF16)) + b_ref[...]


def _adaln_all(cond, w_mod, b_mod):
    depth, d, width = w_mod.shape
    tn = width // 4
    return pl.pallas_call(
        _mod_kernel,
        grid=(depth, width // tn),
        in_specs=[
            pl.BlockSpec((MOD_ROWS, d), lambda l, j: (0, 0)),
            pl.BlockSpec((None, d, tn), lambda l, j: (l, 0, j)),
            pl.BlockSpec((None, 1, tn), lambda l, j: (l, 0, j)),
        ],
        out_specs=pl.BlockSpec((None, MOD_ROWS, tn), lambda l, j: (l, 0, j)),
        out_shape=jax.ShapeDtypeStruct((depth, MOD_ROWS, width), F32),
        compiler_params=_cparams(2),
        name="adaln",
    )(cond, w_mod, b_mod.reshape(depth, 1, width))


def _rms(x, eps=EPS):
    return x * lax.rsqrt(jnp.mean(x * x, axis=-1, keepdims=True) + eps)


def _gelu_tanh(x):
    c = math.sqrt(2.0 / math.pi)
    return 0.5 * x * (1.0 + jnp.tanh(c * (x + 0.044715 * (x * x * x))))


def _head_mean_sq(p, e):
    sq = p * p
    hi = sq.astype(BF16)
    lo = (sq - hi.astype(F32)).astype(BF16)
    blk = min(p.shape[1], e.shape[0])
    eb = e[0:blk, 0:blk]
    cols = [_dot(hi[:, c:c + blk], eb) + _dot(lo[:, c:c + blk], eb)
            for c in range(0, p.shape[1], blk)]
    return cols[0] if len(cols) == 1 else jnp.concatenate(cols, axis=1)


def _rope_slab(t, cos, sa, sb):
    up = pltpu.roll(t, LANES - 16, axis=1)
    dn = pltpu.roll(t, 16, axis=1)
    return t * cos + up * sa + dn * sb


def _proj_kernel(*refs, d, cw, sgw, n_lat, n_rid, split):
    (xa_ref, xb_ref, mod_ref, n1_ref, w_ref, qg_ref, kg_ref, sgn_ref, rope_ref,
     e_ref) = refs[:N_PROJ_IN]
    rid_in = refs[N_PROJ_IN:N_PROJ_IN + n_rid]
    (ab_ref, z_ref, q_ref, kk_ref, vt_ref, gu_ref, svn_ref,
     gate_ref) = refs[N_PROJ_IN + n_rid:N_PROJ_IN + n_rid + N_PROJ_OUT]
    rid_out = refs[N_PROJ_IN + n_rid + N_PROJ_OUT:]
    off_q = 3 * cw
    off_k = off_q + ATTN_WIDTH
    off_v = off_k + KV_WIDTH
    off_u = off_v + KV_WIDTH
    off_g = off_u + 2 * sgw

    for src, dst in zip(rid_in, rid_out):
        dst[...] = src[...].astype(BF16)

    x = xa_ref[...]
    if split:
        x = jnp.where(pl.program_id(0) < n_lat, x, xb_ref[...])
    shift = mod_ref[:, 0:d]
    scale = mod_ref[:, d:2 * d]
    h = (_rms(x) * n1_ref[...]) * (1.0 + scale) + shift
    hb = h.astype(BF16)

    cos = rope_ref[0]
    sa = rope_ref[1]
    sb = rope_ref[2]
    e = e_ref[...]

    pq = _dot(hb, w_ref[:, off_q:off_k])
    qn = pq * lax.rsqrt(_head_mean_sq(pq, e) + EPS) * qg_ref[...]
    for s in range(ATTN_WIDTH // LANES):
        r = _rope_slab(qn[:, s * LANES:(s + 1) * LANES], cos, sa, sb)
        q_ref[:, s * LANES:(s + 1) * LANES] = (r * Q_SCALE).astype(BF16)

    pk = _dot(hb, w_ref[:, off_k:off_v])
    kn = pk * lax.rsqrt(_head_mean_sq(pk, e) + EPS) * kg_ref[...]
    kr = _rope_slab(kn, cos, sa, sb)
    ksw = pltpu.roll(kr, HEAD_DIM, axis=1)
    lane = lax.broadcasted_iota(jnp.int32, kr.shape, 1)
    lo_half = lane < HEAD_DIM
    zero = jnp.zeros_like(kr)
    kk_ref[:, 0 * LANES:1 * LANES] = jnp.where(lo_half, kr, zero).astype(BF16)
    kk_ref[:, 1 * LANES:2 * LANES] = jnp.where(lo_half, zero, ksw).astype(BF16)
    kk_ref[:, 2 * LANES:3 * LANES] = jnp.where(lo_half, ksw, zero).astype(BF16)
    kk_ref[:, 3 * LANES:4 * LANES] = jnp.where(lo_half, zero, kr).astype(BF16)

    pvt = _dot(hb, w_ref[:, off_v:off_u]).T.astype(BF16)
    ones = jnp.ones((ONES_ROWS, pvt.shape[1]), BF16)
    for hh in range(N_KV_HEADS):
        vt_ref[hh * VT_ROWS:hh * VT_ROWS + HEAD_DIM, :] = pvt[hh * HEAD_DIM:(hh + 1) * HEAD_DIM, :]
        vt_ref[hh * VT_ROWS + HEAD_DIM:(hh + 1) * VT_ROWS, :] = ones

    pu = _dot(hb, w_ref[:, off_u:off_g])
    gu_ref[...] = _gelu_tanh(pu[:, 0:sgw])
    svn_ref[...] = (_rms(_gelu_tanh(pu[:, sgw:2 * sgw])) * sgn_ref[...]).astype(BF16)

    for j in range(N_BRANCH):
        pg = _dot(hb, w_ref[:, off_g + j * d:off_g + (j + 1) * d])
        gate_ref[:, j * d:(j + 1) * d] = jax.nn.sigmoid(pg)

    pc = _dot(hb, w_ref[:, 0:off_q])
    ab_ref[...] = pc[:, 0:cw]
    z_ref[...] = pc[:, cw:2 * cw] * pc[:, 2 * cw:3 * cw]


def _proj(xa, xb, mods, l, norm1, w_in, qg, kg, sgn, rope, e, geo, riders):
    split = xb is not None
    d = xa.shape[1]
    in_w = w_in.shape[-1]
    cw = d // 4
    sgw = d // 4
    bsz = geo["B"]
    n_s = geo["n_s"] * TM // TP
    n_lat = bsz * n_s
    n_t = n_lat + bsz * geo["n_l"] * TM // TP
    ntok = n_t * TP

    def rider_blocks(rows):
        nb = 1
        while 2 * nb <= n_t and rows % (2 * nb * 2 * SUBLANES) == 0:
            nb *= 2
        return nb

    n_blks = [rider_blocks(a.shape[1]) for a, _ in riders]

    def mod_row(i):
        return jnp.where(i < n_lat, i // n_s, bsz)

    def rope_blk(i):
        return jnp.where(i < n_lat, i % n_s, n_s)


    row = lambda w: pl.BlockSpec((TP, w), lambda i: (i, 0))
    const = lambda *blk: pl.BlockSpec((None,) + blk, lambda i: (l,) + (0,) * len(blk),
                                      pipeline_mode=pl.Buffered(1))
    outs = [(cw, F32), (cw, F32), (ATTN_WIDTH, BF16), (4 * LANES, BF16), None,
            (sgw, F32), (sgw, BF16), (N_BRANCH * d, F32)]
    vt_rows = N_KV_HEADS * VT_ROWS
    out_specs = [pl.BlockSpec((vt_rows, TP), lambda i: (0, i)) if o is None else row(o[0])
                 for o in outs]
    out_shape = [jax.ShapeDtypeStruct((vt_rows, ntok), BF16) if o is None
                 else jax.ShapeDtypeStruct((ntok, o[0]), o[1]) for o in outs]
    rid_in_specs = [pl.BlockSpec((None, a.shape[1] // nb, a.shape[2]),
                                 lambda i, li=li, nb=nb: (li, jnp.minimum(i, nb - 1), 0))
                    for (a, li), nb in zip(riders, n_blks)]
    out_specs += [pl.BlockSpec((a.shape[1] // nb, a.shape[2]),
                               lambda i, nb=nb: (jnp.minimum(i, nb - 1), 0))
                  for (a, _), nb in zip(riders, n_blks)]
    out_shape += [jax.ShapeDtypeStruct(a.shape[1:], BF16) for a, _ in riders]
    res = pl.pallas_call(
        functools.partial(_proj_kernel, d=d, cw=cw, sgw=sgw, n_lat=n_lat, n_rid=len(riders),
                          split=split),
        grid=(n_t,),
        in_specs=[
            pl.BlockSpec((TP, d), lambda i: (jnp.minimum(i, n_lat - 1) if split else i, 0)),
            pl.BlockSpec((TP, d), lambda i: (jnp.maximum(i - n_lat, 0) if split else 0, 0)),
            pl.BlockSpec((None, None, 1, N_MOD * d), lambda i: (l, mod_row(i), 0, 0)),
            const(1, d),
            pl.BlockSpec((d, in_w), lambda i: (0, 0), pipeline_mode=pl.Buffered(1)),
            const(1, ATTN_WIDTH),
            const(1, KV_WIDTH),
            const(1, sgw),
            pl.BlockSpec((3, TP, LANES), lambda i: (0, rope_blk(i), 0)),
            pl.BlockSpec(e.shape, lambda i: (0, 0), pipeline_mode=pl.Buffered(1)),
        ] + rid_in_specs,
        out_specs=out_specs,
        out_shape=out_shape,
        compiler_params=_cparams(1),
        name="proj",
    )(xa, xb if split else xa, mods, norm1, w_in, qg, kg, sgn, rope, e, *[a for a, _ in riders])
    return res[:N_PROJ_OUT], res[N_PROJ_OUT:]


def _attn_kernel(q_ref, qn_ref, kl_ref, vl_ref, kc_ref, vc_ref, kn_ref, o_ref,
                 m_scr, acc_scr, sa_scr, sb_scr, sc_scr, ma_scr, mb_scr, mc_scr, *, n_lat, s_len):
    i = pl.program_id(0)
    n_steps = s_len // TK
    m_scr[...] = jnp.full(m_scr.shape, -jnp.inf, F32)
    acc_scr[...] = jnp.zeros(acc_scr.shape, F32)

    bufs = {"a": (sa_scr, ma_scr), "b": (sb_scr, mb_scr), "c": (sc_scr, mc_scr)}

    def produce(buf, k_tile, hd, qr=q_ref):
        slab, par, kvh = hd // 2, hd % 2, hd // GROUP
        qs = qr[:, slab * LANES:(slab + 1) * LANES]
        kcol = (2 * kvh + par) * LANES
        s = _dot_t(k_tile[:, kcol:kcol + LANES], qs)
        s_scr, mx_scr = bufs[buf]
        s_scr[hd] = s
        mx_scr[hd] = jnp.max(s, axis=0, keepdims=True)

    def consume(buf, hd, vt_tile):
        s_scr, mx_scr = bufs[buf]
        kvh = hd // GROUP
        m_prev = m_scr[hd]
        m_new = jnp.maximum(m_prev, mx_scr[hd])
        alpha = jnp.exp2(m_prev - m_new)
        p = jnp.exp2(s_scr[hd] - m_new).astype(BF16)
        pv = _dot(vt_tile[kvh * VT_ROWS:(kvh + 1) * VT_ROWS, :], p)
        acc_scr[hd] = acc_scr[hd] * alpha + pv
        m_scr[hd] = m_new

    def step(cur, vt_tile, next_k, nxt, next_q=q_ref):
        if next_k is not None:
            for hd in range(SKEW):
                produce(nxt, next_k, hd, next_q)
        for hd in range(N_Q_HEADS):
            if next_k is not None and hd + SKEW < N_Q_HEADS:
                produce(nxt, next_k, hd + SKEW, next_q)
            consume(cur, hd, vt_tile)

    def k_at(j):
        return kl_ref[pl.ds(pl.multiple_of(j * TK, TK), TK), :]

    def v_at(j):
        return vl_ref[:, pl.ds(pl.multiple_of(j * TK, TK), TK)]

    @pl.when(i == 0)
    def _():
        for hd in range(N_Q_HEADS):
            produce("a", k_at(0), hd)

    @pl.when(i < n_lat)
    def _():
        def body(jj, carry):
            for u in range(UNROLL):
                j = UNROLL * jj + u
                step("ab"[u % 2], v_at(j), k_at(j + 1), "ab"[(u + 1) % 2])
            return carry
        lax.fori_loop(0, n_steps // UNROLL - 1, body, 0)
        for u in range(UNROLL - 1):
            j = n_steps - UNROLL + u
            step("ab"[u % 2], v_at(j), k_at(j + 1), "ab"[(u + 1) % 2])
        step("b", v_at(n_steps - 1), kc_ref[...], "c")

    @pl.when(i >= n_lat)
    def _():
        for hd in range(N_Q_HEADS):
            produce("c", kc_ref[...], hd)

    @pl.when(i < n_lat - 1)
    def _():
        step("c", vc_ref[...], kn_ref[...], "a", qn_ref)

    @pl.when(i >= n_lat - 1)
    def _():
        step("c", vc_ref[...], None, None)

    for slab in range(N_Q_HEADS // 2):
        halves = []
        for hd in (2 * slab, 2 * slab + 1):
            a = acc_scr[hd]
            halves.append(a[0:HEAD_DIM, :] / a[HEAD_DIM:HEAD_DIM + 1, :])
        out = jnp.concatenate(halves, axis=0).T
        o_ref[:, slab * LANES:(slab + 1) * LANES] = out.astype(o_ref.dtype)


def _attn(q, kk, vt, geo, n_tiles):
    n_lat, n_s, n_l, bsz = geo["n_lat"], geo["n_s"], geo["n_l"], geo["B"]
    s_len, l_len = n_s * TM, n_l * TM
    ctx_blk0 = (bsz * s_len) // l_len
    vt_rows = N_KV_HEADS * VT_ROWS

    def lat_b(i):
        return jnp.minimum(i // n_s, bsz - 1)

    def ctx_b(i):
        return ctx_blk0 + jnp.where(i < n_lat, i // n_s, (i - n_lat) // n_l)

    def nxt(i):
        return jnp.minimum(i + 1, n_tiles - 1)

    return pl.pallas_call(
        functools.partial(_attn_kernel, n_lat=n_lat, s_len=s_len),
        grid=(n_tiles,),
        in_specs=[
            pl.BlockSpec((TM, ATTN_WIDTH), lambda i: (i, 0)),
            pl.BlockSpec((TM, ATTN_WIDTH), lambda i: (nxt(i), 0)),
            pl.BlockSpec((s_len, 4 * LANES), lambda i: (lat_b(i), 0)),
            pl.BlockSpec((vt_rows, s_len), lambda i: (0, lat_b(i))),
            pl.BlockSpec((l_len, 4 * LANES), lambda i: (ctx_b(i), 0)),
            pl.BlockSpec((vt_rows, l_len), lambda i: (0, ctx_b(i))),
            pl.BlockSpec((TK, 4 * LANES), lambda i: (lat_b(nxt(i)) * (s_len // TK), 0)),
        ],
        out_specs=pl.BlockSpec((TM, ATTN_WIDTH), lambda i: (i, 0)),
        out_shape=jax.ShapeDtypeStruct((n_tiles * TM, ATTN_WIDTH), BF16),
        scratch_shapes=[pltpu.VMEM((N_Q_HEADS, 1, TM), F32),
                        pltpu.VMEM((N_Q_HEADS, VT_ROWS, TM), F32),
                        pltpu.VMEM((N_Q_HEADS, TK, TM), F32),
                        pltpu.VMEM((N_Q_HEADS, TK, TM), F32),
                        pltpu.VMEM((N_Q_HEADS, l_len, TM), F32),
                        pltpu.VMEM((N_Q_HEADS, 1, TM), F32),
                        pltpu.VMEM((N_Q_HEADS, 1, TM), F32),
                        pltpu.VMEM((N_Q_HEADS, 1, TM), F32)],
        compiler_params=_cparams(1),
        name="attn",
    )(q, q, kk, vt, kk, vt, kk)


def _merge_ffn_kernel(xa_ref, xb_ref, mod_ref, ab_ref, z_ref, zp_ref, zn_ref, at_ref, gu_ref, svn_ref,
                      gate_ref, cw_ref, ws_ref, bt_ref, wa_ref, wb_ref, wc_ref, wo_ref,
                      n2_ref, w1_ref, w3_ref, w2_ref, o_ref,
                      *, d, n_lat, n_s, l_rows, ff_chunk, split):
    i = pl.program_id(0)
    tm = xa_ref.shape[0]
    is_lat = i < n_lat
    t_in_seq = i % n_s

    yb = _dot(at_ref[...], wb_ref[...])

    z = z_ref[...]
    row = lax.broadcasted_iota(jnp.int32, z.shape, 0)
    first = row == jnp.where(is_lat & (t_in_seq == 0), 0, -1)
    last = row == jnp.where(is_lat & (t_in_seq == n_s - 1), tm - 1, -1)
    for k in range(tm // l_rows):
        first = first | (row == jnp.where(is_lat, -1, k * l_rows))
        last = last | (row == jnp.where(is_lat, -1, (k + 1) * l_rows - 1))
    zp = jnp.where(row == 0, zp_ref[SUBLANES - 1:SUBLANES, :], pltpu.roll(z, 1, axis=0))
    zn = jnp.where(row == tm - 1, zn_ref[0:1, :], pltpu.roll(z, tm - 1, axis=0))
    zp = jnp.where(first, jnp.zeros_like(z), zp)
    zn = jnp.where(last, jnp.zeros_like(z), zn)
    conv = zp * cw_ref[0:1, :] + z * cw_ref[1:2, :] + zn * cw_ref[2:3, :]
    ya = _dot((ab_ref[...] * conv).astype(BF16), wa_ref[...])

    sgw = svn_ref.shape[1]
    gw = sgw // SG_GROUPS
    lane = lax.broadcasted_iota(jnp.int32, (CHUNK, sgw), 1)
    parts = []
    for c in range(tm // CHUNK):
        vc = svn_ref[c * CHUNK:(c + 1) * CHUNK, :]
        stacked = jnp.concatenate(
            [jnp.where((lane >= g * gw) & (lane < (g + 1) * gw), vc, jnp.zeros_like(vc))
             for g in range(SG_GROUPS)], axis=0)
        mixed = _dot(ws_ref[...], stacked) + bt_ref[...]
        parts.append(gu_ref[c * CHUNK:(c + 1) * CHUNK, :] * mixed)
    yc = _dot(jnp.concatenate(parts, axis=0).astype(BF16), wc_ref[...])

    y = gate_ref[:, 0:d] * ya + gate_ref[:, d:2 * d] * yb + gate_ref[:, 2 * d:3 * d] * yc
    out = _dot(y.astype(BF16), wo_ref[...])
    x0 = xa_ref[...]
    if split:
        x0 = jnp.where(i < n_lat, x0, xb_ref[...])
    x1 = x0 + mod_ref[:, 2 * d:3 * d] * out

    shift = mod_ref[:, 3 * d:4 * d]
    scale = mod_ref[:, 4 * d:5 * d]
    hb = ((_rms(x1) * n2_ref[...]) * (1.0 + scale) + shift).astype(BF16)
    d_ff = w1_ref.shape[1]
    for r in range(0, tm, FF_ROWS):
        rs = slice(r, r + FF_ROWS)
        acc = None
        for c in range(d_ff // ff_chunk):
            cs = slice(c * ff_chunk, (c + 1) * ff_chunk)
            a = _dot(hb[rs], w1_ref[:, cs])
            b = _dot(hb[rs], w3_ref[:, cs])
            act = ((a * jax.nn.sigmoid(a)) * b).astype(BF16)
            part = _dot(act, w2_ref[cs, :])
            acc = part if acc is None else acc + part
        o_ref[rs, :] = x1[rs] + mod_ref[:, 5 * d:6 * d] * acc


def _merge_ffn(xa, xb, mods, l, ab, z, attn, gu, svn, gate, conv_w, w_s, bias_tab,
               w_a, w_b, w_c, w_o, norm2, w1, w3, w2, geo, n_tiles):
    split = xb is not None
    d = xa.shape[1]
    cw = ab.shape[1]
    sgw = gu.shape[1]
    d_ff = w1.shape[-1]
    ff_chunk = 2 * MXU_DIM if d_ff % (2 * MXU_DIM) == 0 else MXU_DIM
    bsz = geo["B"]
    n_s = geo["n_s"] * TM // TF
    n_lat = bsz * n_s
    n_tiles = n_tiles * TM // TF
    l_rows = geo["n_l"] * TM
    rows_per_tile = TF // SUBLANES
    last_blk = z.shape[0] // SUBLANES - 1

    def mod_row(i):
        return jnp.where(i < n_lat, i // n_s, bsz)

    row = lambda w: pl.BlockSpec((TF, w), lambda i: (i, 0))
    stacked = lambda a, b: pl.BlockSpec((None, a, b), lambda i: (l, 0, 0),
                                        pipeline_mode=pl.Buffered(1))
    whole = lambda a, b: pl.BlockSpec((a, b), lambda i: (0, 0), pipeline_mode=pl.Buffered(1))
    return pl.pallas_call(
        functools.partial(_merge_ffn_kernel, d=d, n_lat=n_lat, n_s=n_s, l_rows=l_rows,
                          ff_chunk=ff_chunk, split=split),
        grid=(n_tiles,),
        in_specs=[
            pl.BlockSpec((TF, d), lambda i: (jnp.minimum(i, n_lat - 1) if split else i, 0)),
            pl.BlockSpec((TF, d), lambda i: (jnp.maximum(i - n_lat, 0) if split else 0, 0)),
            pl.BlockSpec((None, None, 1, N_MOD * d), lambda i: (l, mod_row(i), 0, 0)),
            row(cw),
            row(cw),
            pl.BlockSpec((SUBLANES, cw), lambda i: (jnp.maximum(i * rows_per_tile - 1, 0), 0)),
            pl.BlockSpec((SUBLANES, cw),
                         lambda i: (jnp.minimum((i + 1) * rows_per_tile, last_blk), 0)),
            row(ATTN_WIDTH),
            row(sgw),
            row(sgw),
            row(N_BRANCH * d),
            stacked(CONV_K, cw),
            stacked(CHUNK, SG_GROUPS * CHUNK),
            stacked(CHUNK, sgw),
            whole(cw, d),
            whole(ATTN_WIDTH, d),
            whole(sgw, d),
            whole(d, d),
            stacked(1, d),
            whole(d, d_ff),
            whole(d, d_ff),
            whole(d_ff, d),
        ],
        out_specs=row(d),
        out_shape=jax.ShapeDtypeStruct((n_tiles * TF, d), F32),
        compiler_params=_cparams(1),
        name="merge_ffn",
    )(xa, xb if split else xa, mods, ab, z, z, z, attn, gu, svn, gate, conv_w, w_s, bias_tab,
      w_a, w_b, w_c, w_o, norm2, w1, w3, w2)


def _rope_table(s_len, pad):
    t = jnp.arange(s_len, dtype=jnp.int32)
    pos = jnp.stack([(t // GRID_W).astype(F32), (t % GRID_W).astype(F32)], axis=1)
    inv_freq = ROPE_THETA ** (-jnp.arange(0, AXIS_DIM, 2, dtype=F32) / AXIS_DIM)
    ang = pos[:, :, None] * inv_freq
    cos, sin = jnp.cos(ang), jnp.sin(ang)
    zero = jnp.zeros_like(sin)
    c64 = jnp.stack([cos, cos], axis=2).reshape(s_len, HEAD_DIM)
    a64 = jnp.stack([-sin, zero], axis=2).reshape(s_len, HEAD_DIM)
    b64 = jnp.stack([zero, sin], axis=2).reshape(s_len, HEAD_DIM)
    tab = jnp.stack([c64, a64, b64], axis=0)
    ident = jnp.stack([jnp.ones((pad, HEAD_DIM), F32), jnp.zeros((pad, HEAD_DIM), F32),
                       jnp.zeros((pad, HEAD_DIM), F32)], axis=0)
    tab = jnp.concatenate([tab, ident], axis=1)
    return jnp.tile(tab, (1, 1, LANES // HEAD_DIM))


def _head_mean_matrix():
    r = jnp.arange(MXU_DIM) // HEAD_DIM
    return jnp.where(r[:, None] == r[None, :], 1.0 / HEAD_DIM, 0.0).astype(BF16)


def kernel(x, c, ctx, c_ctx, w_mod, b_mod, norm1, w_in, q_gain, k_gain, conv_w, sg_norm,
           w_s, b_s, w_a, w_b, w_c, w_o, norm2, w_ff1, w_ff3, w_ff2):
    bsz, s_len, d = x.shape
    l_len = ctx.shape[1]
    depth = w_mod.shape[0]
    sgw = sg_norm.shape[1]
    assert s_len % TM == 0 and l_len % TM == 0 and s_len % (UNROLL * TK) == 0 and s_len % GRID_W == 0
    assert (bsz * s_len) % l_len == 0 and bsz + 1 <= MOD_ROWS
    assert s_len % TP == 0 and (bsz * l_len) % TP == 0 and TP % TM == 0
    assert s_len % TF == 0 and (bsz * l_len) % TF == 0 and TF % TM == 0 and TF % l_len == 0
    n_s, n_l = s_len // TM, l_len // TM
    geo = {"B": bsz, "n_s": n_s, "n_l": n_l, "n_lat": bsz * n_s}
    n_all = bsz * (n_s + n_l)

    cond = jnp.concatenate(
        [c, c_ctx[None, :], jnp.zeros((MOD_ROWS - bsz - 1, d), F32)], axis=0)
    mods = _adaln_all(cond, w_mod, b_mod).reshape(depth, MOD_ROWS, 1, N_MOD * d)

    rope = _rope_table(s_len, TP)
    e = _head_mean_matrix()
    qg = jnp.tile(q_gain, (1, N_Q_HEADS)).reshape(depth, 1, ATTN_WIDTH)
    kg = jnp.tile(k_gain, (1, N_KV_HEADS)).reshape(depth, 1, KV_WIDTH)
    sgn = sg_norm.reshape(depth, 1, sgw)
    n1 = norm1.reshape(depth, 1, d)
    n2 = norm2.reshape(depth, 1, d)
    bias_tab = jnp.repeat(jnp.swapaxes(b_s, 1, 2), sgw // SG_GROUPS, axis=2)
    w_s_b = jnp.transpose(w_s, (0, 2, 1, 3)).reshape(depth, CHUNK, SG_GROUPS * CHUNK).astype(BF16)
    later = [w_a, w_b, w_c, w_o, w_ff1, w_ff3, w_ff2]
    w_in_l = w_in[0].astype(BF16)

    xa, xb = x.reshape(bsz * s_len, d), ctx.reshape(bsz * l_len, d)
    for l in range(depth):
        last = l == depth - 1
        n_tiles = geo["n_lat"] if last else n_all
        riders = [(w, l) for w in later] + ([] if last else [(w_in, l + 1)])
        (ab, z, q, kk, vt, gu, svn, gate), cast = _proj(
            xa, xb, mods, l, n1, w_in_l, qg, kg, sgn, rope, e, geo, riders)
        wa_l, wb_l, wc_l, wo_l, w1_l, w3_l, w2_l = cast[:len(later)]
        if not last:
            w_in_l = cast[len(later)]
        attn = _attn(q, kk, vt, geo, n_tiles)
        xa = _merge_ffn(xa, xb, mods, l, ab, z, attn, gu, svn, gate, conv_w, w_s_b,
                        bias_tab, wa_l, wb_l, wc_l, wo_l, n2, w1_l, w3_l, w2_l, geo, n_tiles)
        xb = None
    return xa.reshape(bsz, s_len, d)
```

```python
import functools
import math

import jax
import jax.numpy as jnp
import numpy as np
from jax import lax
from jax.experimental import pallas as pl
from jax.experimental.pallas import tpu as pltpu

F32 = jnp.float32
BF16 = jnp.bfloat16

GRID_W = 64
HEAD_DIM = 64
N_Q_HEADS = 8
N_KV_HEADS = 2
GROUP = N_Q_HEADS // N_KV_HEADS
ATTN_WIDTH = N_Q_HEADS * HEAD_DIM
KV_WIDTH = N_KV_HEADS * HEAD_DIM
AXIS_DIM = HEAD_DIM // 2
ROPE_THETA = 10000.0
CONV_K = 3
CHUNK = 128
SG_GROUPS = 4
N_BRANCH = 3
N_MOD = 6
EPS = 1e-6

LANES = 128
MXU_DIM = 256
SUBLANES = 8
TM = 256
TP = 512
TF = 512
FF_ROWS = 512
TK = 512
ONES_ROWS = 16
VT_ROWS = HEAD_DIM + ONES_ROWS
Q_SCALE = HEAD_DIM ** -0.5 * math.log2(math.e)
SKEW = 2
UNROLL = 4
MOD_ROWS = 8
N_PROJ_IN = 10
N_PROJ_OUT = 8
VMEM_LIMIT = 56 * 1024 * 1024


def _cparams(n_axes=1):
    return pltpu.CompilerParams(dimension_semantics=("arbitrary",) * n_axes,
                                vmem_limit_bytes=VMEM_LIMIT)


def _dot(a, b):
    return jnp.dot(a, b, preferred_element_type=F32)


def _dot_t(a, b):
    return lax.dot_general(a, b, (((1,), (1,)), ((), ())), preferred_element_type=F32)


def _mod_kernel(cond_ref, w_ref, b_ref, o_ref):
    cnd = cond_ref[...]
    s = (cnd * jax.nn.sigmoid(cnd)).astype(BF16)
    o_ref[...] = _dot(s, w_ref[...].astype(BF16)) + b_ref[...]


def _adaln_all(cond, w_mod, b_mod):
    depth, d, width = w_mod.shape
    tn = width // 4
    return pl.pallas_call(
        _mod_kernel,
        grid=(depth, width // tn),
        in_specs=[
            pl.BlockSpec((MOD_ROWS, d), lambda l, j: (0, 0)),
            pl.BlockSpec((None, d, tn), lambda l, j: (l, 0, j)),
            pl.BlockSpec((None, 1, tn), lambda l, j: (l, 0, j)),
        ],
        out_specs=pl.BlockSpec((None, MOD_ROWS, tn), lambda l, j: (l, 0, j)),
        out_shape=jax.ShapeDtypeStruct((depth, MOD_ROWS, width), F32),
        compiler_params=_cparams(2),
        name="adaln",
    )(cond, w_mod, b_mod.reshape(depth, 1, width))


def _rms(x, eps=EPS):
    return x * lax.rsqrt(jnp.mean(x * x, axis=-1, keepdims=True) + eps)


def _gelu_tanh(x):
    c = math.sqrt(2.0 / math.pi)
    return 0.5 * x * (1.0 + jnp.tanh(c * (x + 0.044715 * (x * x * x))))


def _head_mean_sq(p, e):
    sq = p * p
    hi = sq.astype(BF16)
    lo = (sq - hi.astype(F32)).astype(BF16)
    blk = min(p.shape[1], e.shape[0])
    eb = e[0:blk, 0:blk]
    cols = [_dot(hi[:, c:c + blk], eb) + _dot(lo[:, c:c + blk], eb)
            for c in range(0, p.shape[1], blk)]
    return cols[0] if len(cols) == 1 else jnp.concatenate(cols, axis=1)


def _rope_slab(t, cos, sa, sb):
    up = pltpu.roll(t, LANES - 16, axis=1)
    dn = pltpu.roll(t, 16, axis=1)
    return t * cos + up * sa + dn * sb


def _proj_kernel(*refs, d, cw, sgw, n_lat, n_rid, split):
    (xa_ref, xb_ref, mod_ref, n1_ref, w_ref, qg_ref, kg_ref, sgn_ref, rope_ref,
     e_ref) = refs[:N_PROJ_IN]
    rid_in = refs[N_PROJ_IN:N_PROJ_IN + n_rid]
    (ab_ref, z_ref, q_ref, kk_ref, vt_ref, gu_ref, svn_ref,
     gate_ref) = refs[N_PROJ_IN + n_rid:N_PROJ_IN + n_rid + N_PROJ_OUT]
    rid_out = refs[N_PROJ_IN + n_rid + N_PROJ_OUT:]
    off_q = 3 * cw
    off_k = off_q + ATTN_WIDTH
    off_v = off_k + KV_WIDTH
    off_u = off_v + KV_WIDTH
    off_g = off_u + 2 * sgw

    for src, dst in zip(rid_in, rid_out):
        dst[...] = src[...].astype(BF16)

    x = xa_ref[...]
    if split:
        x = jnp.where(pl.program_id(0) < n_lat, x, xb_ref[...])
    shift = mod_ref[:, 0:d]
    scale = mod_ref[:, d:2 * d]
    h = (_rms(x) * n1_ref[...]) * (1.0 + scale) + shift
    hb = h.astype(BF16)

    cos = rope_ref[0]
    sa = rope_ref[1]
    sb = rope_ref[2]
    e = e_ref[...]

    pq = _dot(hb, w_ref[:, off_q:off_k])
    qn = pq * lax.rsqrt(_head_mean_sq(pq, e) + EPS) * qg_ref[...]
    for s in range(ATTN_WIDTH // LANES):
        r = _rope_slab(qn[:, s * LANES:(s + 1) * LANES], cos, sa, sb)
        q_ref[:, s * LANES:(s + 1) * LANES] = (r * Q_SCALE).astype(BF16)

    pk = _dot(hb, w_ref[:, off_k:off_v])
    kn = pk * lax.rsqrt(_head_mean_sq(pk, e) + EPS) * kg_ref[...]
    kr = _rope_slab(kn, cos, sa, sb)
    ksw = pltpu.roll(kr, HEAD_DIM, axis=1)
    lane = lax.broadcasted_iota(jnp.int32, kr.shape, 1)
    lo_half = lane < HEAD_DIM
    zero = jnp.zeros_like(kr)
    kk_ref[:, 0 * LANES:1 * LANES] = jnp.where(lo_half, kr, zero).astype(BF16)
    kk_ref[:, 1 * LANES:2 * LANES] = jnp.where(lo_half, zero, ksw).astype(BF16)
    kk_ref[:, 2 * LANES:3 * LANES] = jnp.where(lo_half, ksw, zero).astype(BF16)
    kk_ref[:, 3 * LANES:4 * LANES] = jnp.where(lo_half, zero, kr).astype(BF16)

    pvt = _dot(hb, w_ref[:, off_v:off_u]).T.astype(BF16)
    ones = jnp.ones((ONES_ROWS, pvt.shape[1]), BF16)
    for hh in range(N_KV_HEADS):
        vt_ref[hh * VT_ROWS:hh * VT_ROWS + HEAD_DIM, :] = pvt[hh * HEAD_DIM:(hh + 1) * HEAD_DIM, :]
        vt_ref[hh * VT_ROWS + HEAD_DIM:(hh + 1) * VT_ROWS, :] = ones

    pu = _dot(hb, w_ref[:, off_u:off_g])
    gu_ref[...] = _gelu_tanh(pu[:, 0:sgw])
    svn_ref[...] = (_rms(_gelu_tanh(pu[:, sgw:2 * sgw])) * sgn_ref[...]).astype(BF16)

    for j in range(N_BRANCH):
        pg = _dot(hb, w_ref[:, off_g + j * d:off_g + (j + 1) * d])
        gate_ref[:, j * d:(j + 1) * d] = jax.nn.sigmoid(pg)

    pc = _dot(hb, w_ref[:, 0:off_q])
    ab_ref[...] = pc[:, 0:cw]
    z_ref[...] = pc[:, cw:2 * cw] * pc[:, 2 * cw:3 * cw]


def _proj(xa, xb, mods, l, norm1, w_in, qg, kg, sgn, rope, e, geo, riders):
    split = xb is not None
    d = xa.shape[1]
    in_w = w_in.shape[-1]
    cw = d // 4
    sgw = d // 4
    bsz = geo["B"]
    n_s = geo["n_s"] * TM // TP
    n_lat = bsz * n_s
    n_t = n_lat + bsz * geo["n_l"] * TM // TP
    ntok = n_t * TP

    def rider_blocks(rows):
        nb = 1
        while 2 * nb <= n_t and rows % (2 * nb * 2 * SUBLANES) == 0:
            nb *= 2
        return nb

    n_blks = [rider_blocks(a.shape[1]) for a, _ in riders]

    def mod_row(i):
        return jnp.where(i < n_lat, i // n_s, bsz)

    def rope_blk(i):
        return jnp.where(i < n_lat, i % n_s, n_s)


    row = lambda w: pl.BlockSpec((TP, w), lambda i: (i, 0))
    const = lambda *blk: pl.BlockSpec((None,) + blk, lambda i: (l,) + (0,) * len(blk),
                                      pipeline_mode=pl.Buffered(1))
    outs = [(cw, F32), (cw, F32), (ATTN_WIDTH, BF16), (4 * LANES, BF16), None,
            (sgw, F32), (sgw, BF16), (N_BRANCH * d, F32)]
    vt_rows = N_KV_HEADS * VT_ROWS
    out_specs = [pl.BlockSpec((vt_rows, TP), lambda i: (0, i)) if o is None else row(o[0])
                 for o in outs]
    out_shape = [jax.ShapeDtypeStruct((vt_rows, ntok), BF16) if o is None
                 else jax.ShapeDtypeStruct((ntok, o[0]), o[1]) for o in outs]
    rid_in_specs = [pl.BlockSpec((None, a.shape[1] // nb, a.shape[2]),
                                 lambda i, li=li, nb=nb: (li, jnp.minimum(i, nb - 1), 0))
                    for (a, li), nb in zip(riders, n_blks)]
    out_specs += [pl.BlockSpec((a.shape[1] // nb, a.shape[2]),
                               lambda i, nb=nb: (jnp.minimum(i, nb - 1), 0))
                  for (a, _), nb in zip(riders, n_blks)]
    out_shape += [jax.ShapeDtypeStruct(a.shape[1:], BF16) for a, _ in riders]
    res = pl.pallas_call(
        functools.partial(_proj_kernel, d=d, cw=cw, sgw=sgw, n_lat=n_lat, n_rid=len(riders),
                          split=split),
        grid=(n_t,),
        in_specs=[
            pl.BlockSpec((TP, d), lambda i: (jnp.minimum(i, n_lat - 1) if split else i, 0)),
            pl.BlockSpec((TP, d), lambda i: (jnp.maximum(i - n_lat, 0) if split else 0, 0)),
            pl.BlockSpec((None, None, 1, N_MOD * d), lambda i: (l, mod_row(i), 0, 0)),
            const(1, d),
            pl.BlockSpec((d, in_w), lambda i: (0, 0), pipeline_mode=pl.Buffered(1)),
            const(1, ATTN_WIDTH),
            const(1, KV_WIDTH),
            const(1, sgw),
            pl.BlockSpec((3, TP, LANES), lambda i: (0, rope_blk(i), 0)),
            pl.BlockSpec(e.shape, lambda i: (0, 0), pipeline_mode=pl.Buffered(1)),
        ] + rid_in_specs,
        out_specs=out_specs,
        out_shape=out_shape,
        compiler_params=_cparams(1),
        name="proj",
    )(xa, xb if split else xa, mods, norm1, w_in, qg, kg, sgn, rope, e, *[a for a, _ in riders])
    return res[:N_PROJ_OUT], res[N_PROJ_OUT:]


def _attn_kernel(q_ref, qn_ref, kl_ref, vl_ref, kc_ref, vc_ref, kn_ref, o_ref,
                 m_scr, acc_scr, sa_scr, sb_scr, sc_scr, ma_scr, mb_scr, mc_scr, *, n_lat, s_len):
    i = pl.program_id(0)
    n_steps = s_len // TK
    m_scr[...] = jnp.full(m_scr.shape, -jnp.inf, F32)
    acc_scr[...] = jnp.zeros(acc_scr.shape, F32)

    bufs = {"a": (sa_scr, ma_scr), "b": (sb_scr, mb_scr), "c": (sc_scr, mc_scr)}

    def produce(buf, k_tile, hd, qr=q_ref):
        slab, par, kvh = hd // 2, hd % 2, hd // GROUP
        qs = qr[:, slab * LANES:(slab + 1) * LANES]
        kcol = (2 * kvh + par) * LANES
        s = _dot_t(k_tile[:, kcol:kcol + LANES], qs)
        s_scr, mx_scr = bufs[buf]
        s_scr[hd] = s
        mx_scr[hd] = jnp.max(s, axis=0, keepdims=True)

    def consume(buf, hd, vt_tile):
        s_scr, mx_scr = bufs[buf]
        kvh = hd // GROUP
        m_prev = m_scr[hd]
        m_new = jnp.maximum(m_prev, mx_scr[hd])
        alpha = jnp.exp2(m_prev - m_new)
        p = jnp.exp2(s_scr[hd] - m_new).astype(BF16)
        pv = _dot(vt_tile[kvh * VT_ROWS:(kvh + 1) * VT_ROWS, :], p)
        acc_scr[hd] = acc_scr[hd] * alpha + pv
        m_scr[hd] = m_new

    def step(cur, vt_tile, next_k, nxt, next_q=q_ref):
        if next_k is not None:
            for hd in range(SKEW):
                produce(nxt, next_k, hd, next_q)
        for hd in range(N_Q_HEADS):
            if next_k is not None and hd + SKEW < N_Q_HEADS:
                produce(nxt, next_k, hd + SKEW, next_q)
            consume(cur, hd, vt_tile)

    def k_at(j):
        return kl_ref[pl.ds(pl.multiple_of(j * TK, TK), TK), :]

    def v_at(j):
        return vl_ref[:, pl.ds(pl.multiple_of(j * TK, TK), TK)]

    @pl.when(i == 0)
    def _():
        for hd in range(N_Q_HEADS):
            produce("a", k_at(0), hd)

    @pl.when(i < n_lat)
    def _():
        def body(jj, carry):
            for u in range(UNROLL):
                j = UNROLL * jj + u
                step("ab"[u % 2], v_at(j), k_at(j + 1), "ab"[(u + 1) % 2])
            return carry
        lax.fori_loop(0, n_steps // UNROLL - 1, body, 0)
        for u in range(UNROLL - 1):
            j = n_steps - UNROLL + u
            step("ab"[u % 2], v_at(j), k_at(j + 1), "ab"[(u + 1) % 2])
        step("b", v_at(n_steps - 1), kc_ref[...], "c")

    @pl.when(i >= n_lat)
    def _():
        for hd in range(N_Q_HEADS):
            produce("c", kc_ref[...], hd)

    @pl.when(i < n_lat - 1)
    def _():
        step("c", vc_ref[...], kn_ref[...], "a", qn_ref)

    @pl.when(i >= n_lat - 1)
    def _():
        step("c", vc_ref[...], None, None)

    for slab in range(N_Q_HEADS // 2):
        halves = []
        for hd in (2 * slab, 2 * slab + 1):
            a = acc_scr[hd]
            halves.append(a[0:HEAD_DIM, :] / a[HEAD_DIM:HEAD_DIM + 1, :])
        out = jnp.concatenate(halves, axis=0).T
        o_ref[:, slab * LANES:(slab + 1) * LANES] = out.astype(o_ref.dtype)


def _attn(q, kk, vt, geo, n_tiles):
    n_lat, n_s, n_l, bsz = geo["n_lat"], geo["n_s"], geo["n_l"], geo["B"]
    s_len, l_len = n_s * TM, n_l * TM
    ctx_blk0 = (bsz * s_len) // l_len
    vt_rows = N_KV_HEADS * VT_ROWS

    def lat_b(i):
        return jnp.minimum(i // n_s, bsz - 1)

    def ctx_b(i):
        return ctx_blk0 + jnp.where(i < n_lat, i // n_s, (i - n_lat) // n_l)

    def nxt(i):
        return jnp.minimum(i + 1, n_tiles - 1)

    return pl.pallas_call(
        functools.partial(_attn_kernel, n_lat=n_lat, s_len=s_len),
        grid=(n_tiles,),
        in_specs=[
            pl.BlockSpec((TM, ATTN_WIDTH), lambda i: (i, 0)),
            pl.BlockSpec((TM, ATTN_WIDTH), lambda i: (nxt(i), 0)),
            pl.BlockSpec((s_len, 4 * LANES), lambda i: (lat_b(i), 0)),
            pl.BlockSpec((vt_rows, s_len), lambda i: (0, lat_b(i))),
            pl.BlockSpec((l_len, 4 * LANES), lambda i: (ctx_b(i), 0)),
            pl.BlockSpec((vt_rows, l_len), lambda i: (0, ctx_b(i))),
            pl.BlockSpec((TK, 4 * LANES), lambda i: (lat_b(nxt(i)) * (s_len // TK), 0)),
        ],
        out_specs=pl.BlockSpec((TM, ATTN_WIDTH), lambda i: (i, 0)),
        out_shape=jax.ShapeDtypeStruct((n_tiles * TM, ATTN_WIDTH), BF16),
        scratch_shapes=[pltpu.VMEM((N_Q_HEADS, 1, TM), F32),
                        pltpu.VMEM((N_Q_HEADS, VT_ROWS, TM), F32),
                        pltpu.VMEM((N_Q_HEADS, TK, TM), F32),
                        pltpu.VMEM((N_Q_HEADS, TK, TM), F32),
                        pltpu.VMEM((N_Q_HEADS, l_len, TM), F32),
                        pltpu.VMEM((N_Q_HEADS, 1, TM), F32),
                        pltpu.VMEM((N_Q_HEADS, 1, TM), F32),
                        pltpu.VMEM((N_Q_HEADS, 1, TM), F32)],
        compiler_params=_cparams(1),
        name="attn",
    )(q, q, kk, vt, kk, vt, kk)


def _merge_ffn_kernel(xa_ref, xb_ref, mod_ref, ab_ref, z_ref, zp_ref, zn_ref, at_ref, gu_ref, svn_ref,
                      gate_ref, cw_ref, ws_ref, bt_ref, wa_ref, wb_ref, wc_ref, wo_ref,
                      n2_ref, w1_ref, w3_ref, w2_ref, o_ref,
                      *, d, n_lat, n_s, l_rows, ff_chunk, split):
    i = pl.program_id(0)
    tm = xa_ref.shape[0]
    is_lat = i < n_lat
    t_in_seq = i % n_s

    yb = _dot(at_ref[...], wb_ref[...])

    z = z_ref[...]
    row = lax.broadcasted_iota(jnp.int32, z.shape, 0)
    first = row == jnp.where(is_lat & (t_in_seq == 0), 0, -1)
    last = row == jnp.where(is_lat & (t_in_seq == n_s - 1), tm - 1, -1)
    for k in range(tm // l_rows):
        first = first | (row == jnp.where(is_lat, -1, k * l_rows))
        last = last | (row == jnp.where(is_lat, -1, (k + 1) * l_rows - 1))
    zp = jnp.where(row == 0, zp_ref[SUBLANES - 1:SUBLANES, :], pltpu.roll(z, 1, axis=0))
    zn = jnp.where(row == tm - 1, zn_ref[0:1, :], pltpu.roll(z, tm - 1, axis=0))
    zp = jnp.where(first, jnp.zeros_like(z), zp)
    zn = jnp.where(last, jnp.zeros_like(z), zn)
    conv = zp * cw_ref[0:1, :] + z * cw_ref[1:2, :] + zn * cw_ref[2:3, :]
    ya = _dot((ab_ref[...] * conv).astype(BF16), wa_ref[...])

    sgw = svn_ref.shape[1]
    gw = sgw // SG_GROUPS
    lane = lax.broadcasted_iota(jnp.int32, (CHUNK, sgw), 1)
    parts = []
    for c in range(tm // CHUNK):
        vc = svn_ref[c * CHUNK:(c + 1) * CHUNK, :]
        stacked = jnp.concatenate(
            [jnp.where((lane >= g * gw) & (lane < (g + 1) * gw), vc, jnp.zeros_like(vc))
             for g in range(SG_GROUPS)], axis=0)
        mixed = _dot(ws_ref[...], stacked) + bt_ref[...]
        parts.append(gu_ref[c * CHUNK:(c + 1) * CHUNK, :] * mixed)
    yc = _dot(jnp.concatenate(parts, axis=0).astype(BF16), wc_ref[...])

    y = gate_ref[:, 0:d] * ya + gate_ref[:, d:2 * d] * yb + gate_ref[:, 2 * d:3 * d] * yc
    out = _dot(y.astype(BF16), wo_ref[...])
    x0 = xa_ref[...]
    if split:
        x0 = jnp.where(i < n_lat, x0, xb_ref[...])
    x1 = x0 + mod_ref[:, 2 * d:3 * d] * out

    shift = mod_ref[:, 3 * d:4 * d]
    scale = mod_ref[:, 4 * d:5 * d]
    hb = ((_rms(x1) * n2_ref[...]) * (1.0 + scale) + shift).astype(BF16)
    d_ff = w1_ref.shape[1]
    for r in range(0, tm, FF_ROWS):
        rs = slice(r, r + FF_ROWS)
        acc = None
        for c in range(d_ff // ff_chunk):
            cs = slice(c * ff_chunk, (c + 1) * ff_chunk)
            a = _dot(hb[rs], w1_ref[:, cs])
            b = _dot(hb[rs], w3_ref[:, cs])
            act = ((a * jax.nn.sigmoid(a)) * b).astype(BF16)
            part = _dot(act, w2_ref[cs, :])
            acc = part if acc is None else acc + part
        o_ref[rs, :] = x1[rs] + mod_ref[:, 5 * d:6 * d] * acc


def _merge_ffn(xa, xb, mods, l, ab, z, attn, gu, svn, gate, conv_w, w_s, bias_tab,
               w_a, w_b, w_c, w_o, norm2, w1, w3, w2, geo, n_tiles):
    split = xb is not None
    d = xa.shape[1]
    cw = ab.shape[1]
    sgw = gu.shape[1]
    d_ff = w1.shape[-1]
    ff_chunk = 2 * MXU_DIM if d_ff % (2 * MXU_DIM) == 0 else MXU_DIM
    bsz = geo["B"]
    n_s = geo["n_s"] * TM // TF
    n_lat = bsz * n_s
    n_tiles = n_tiles * TM // TF
    l_rows = geo["n_l"] * TM
    rows_per_tile = TF // SUBLANES
    last_blk = z.shape[0] // SUBLANES - 1

    def mod_row(i):
        return jnp.where(i < n_lat, i // n_s, bsz)

    row = lambda w: pl.BlockSpec((TF, w), lambda i: (i, 0))
    stacked = lambda a, b: pl.BlockSpec((None, a, b), lambda i: (l, 0, 0),
                                        pipeline_mode=pl.Buffered(1))
    whole = lambda a, b: pl.BlockSpec((a, b), lambda i: (0, 0), pipeline_mode=pl.Buffered(1))
    return pl.pallas_call(
        functools.partial(_merge_ffn_kernel, d=d, n_lat=n_lat, n_s=n_s, l_rows=l_rows,
                          ff_chunk=ff_chunk, split=split),
        grid=(n_tiles,),
        in_specs=[
            pl.BlockSpec((TF, d), lambda i: (jnp.minimum(i, n_lat - 1) if split else i, 0)),
            pl.BlockSpec((TF, d), lambda i: (jnp.maximum(i - n_lat, 0) if split else 0, 0)),
            pl.BlockSpec((None, None, 1, N_MOD * d), lambda i: (l, mod_row(i), 0, 0)),
            row(cw),
            row(cw),
            pl.BlockSpec((SUBLANES, cw), lambda i: (jnp.maximum(i * rows_per_tile - 1, 0), 0)),
            pl.BlockSpec((SUBLANES, cw),
                         lambda i: (jnp.minimum((i + 1) * rows_per_tile, last_blk), 0)),
            row(ATTN_WIDTH),
            row(sgw),
            row(sgw),
            row(N_BRANCH * d),
            stacked(CONV_K, cw),
            stacked(CHUNK, SG_GROUPS * CHUNK),
            stacked(CHUNK, sgw),
            whole(cw, d),
            whole(ATTN_WIDTH, d),
            whole(sgw, d),
            whole(d, d),
            stacked(1, d),
            whole(d, d_ff),
            whole(d, d_ff),
            whole(d_ff, d),
        ],
        out_specs=row(d),
        out_shape=jax.ShapeDtypeStruct((n_tiles * TF, d), F32),
        compiler_params=_cparams(1),
        name="merge_ffn",
    )(xa, xb if split else xa, mods, ab, z, z, z, attn, gu, svn, gate, conv_w, w_s, bias_tab,
      w_a, w_b, w_c, w_o, norm2, w1, w3, w2)


def _rope_table(s_len, pad):
    t = np.arange(s_len)
    pos = np.stack([t // GRID_W, t % GRID_W], axis=1).astype(np.float32)
    inv_freq = (ROPE_THETA ** (-np.arange(0, AXIS_DIM, 2, dtype=np.float32) / AXIS_DIM)
                ).astype(np.float32)
    ang = pos[:, :, None] * inv_freq
    cos, sin = np.cos(ang), np.sin(ang)
    zero = np.zeros_like(sin)
    c64 = np.stack([cos, cos], axis=2).reshape(s_len, HEAD_DIM)
    a64 = np.stack([-sin, zero], axis=2).reshape(s_len, HEAD_DIM)
    b64 = np.stack([zero, sin], axis=2).reshape(s_len, HEAD_DIM)
    tab = np.stack([c64, a64, b64], axis=0)
    ident = np.stack([np.ones((pad, HEAD_DIM), np.float32), np.zeros((pad, HEAD_DIM), np.float32),
                      np.zeros((pad, HEAD_DIM), np.float32)], axis=0)
    tab = np.concatenate([tab, ident], axis=1)
    return jnp.asarray(np.tile(tab, (1, 1, LANES // HEAD_DIM)).astype(np.float32))


def _head_mean_matrix():
    r = jnp.arange(MXU_DIM) // HEAD_DIM
    return jnp.where(r[:, None] == r[None, :], 1.0 / HEAD_DIM, 0.0).astype(BF16)


def kernel(x, c, ctx, c_ctx, w_mod, b_mod, norm1, w_in, q_gain, k_gain, conv_w, sg_norm,
           w_s, b_s, w_a, w_b, w_c, w_o, norm2, w_ff1, w_ff3, w_ff2):
    bsz, s_len, d = x.shape
    l_len = ctx.shape[1]
    depth = w_mod.shape[0]
    sgw = sg_norm.shape[1]
    assert s_len % TM == 0 and l_len % TM == 0 and s_len % (UNROLL * TK) == 0 and s_len % GRID_W == 0
    assert (bsz * s_len) % l_len == 0 and bsz + 1 <= MOD_ROWS
    assert s_len % TP == 0 and (bsz * l_len) % TP == 0 and TP % TM == 0
    assert s_len % TF == 0 and (bsz * l_len) % TF == 0 and TF % TM == 0 and TF % l_len == 0
    n_s, n_l = s_len // TM, l_len // TM
    geo = {"B": bsz, "n_s": n_s, "n_l": n_l, "n_lat": bsz * n_s}
    n_all = bsz * (n_s + n_l)

    cond = jnp.concatenate(
        [c, c_ctx[None, :], jnp.zeros((MOD_ROWS - bsz - 1, d), F32)], axis=0)
    mods = _adaln_all(cond, w_mod, b_mod).reshape(depth, MOD_ROWS, 1, N_MOD * d)

    rope = _rope_table(s_len, TP)
    e = _head_mean_matrix()
    qg = jnp.tile(q_gain, (1, N_Q_HEADS)).reshape(depth, 1, ATTN_WIDTH)
    kg = jnp.tile(k_gain, (1, N_KV_HEADS)).reshape(depth, 1, KV_WIDTH)
    sgn = sg_norm.reshape(depth, 1, sgw)
    n1 = norm1.reshape(depth, 1, d)
    n2 = norm2.reshape(depth, 1, d)
    bias_tab = jnp.repeat(jnp.swapaxes(b_s, 1, 2), sgw // SG_GROUPS, axis=2)
    w_s_b = jnp.transpose(w_s, (0, 2, 1, 3)).reshape(depth, CHUNK, SG_GROUPS * CHUNK).astype(BF16)
    later = [w_a, w_b, w_c, w_o, w_ff1, w_ff3, w_ff2]
    w_in_l = w_in[0].astype(BF16)

    xa, xb = x.reshape(bsz * s_len, d), ctx.reshape(bsz * l_len, d)
    for l in range(depth):
        last = l == depth - 1
        n_tiles = geo["n_lat"] if last else n_all
        riders = [(w, l) for w in later] + ([] if last else [(w_in, l + 1)])
        (ab, z, q, kk, vt, gu, svn, gate), cast = _proj(
            xa, xb, mods, l, n1, w_in_l, qg, kg, sgn, rope, e, geo, riders)
        wa_l, wb_l, wc_l, wo_l, w1_l, w3_l, w2_l = cast[:len(later)]
        if not last:
            w_in_l = cast[len(later)]
        attn = _attn(q, kk, vt, geo, n_tiles)
        xa = _merge_ffn(xa, xb, mods, l, ab, z, attn, gu, svn, gate, conv_w, w_s_b,
                        bias_tab, wa_l, wb_l, wc_l, wo_l, n2, w1_l, w3_l, w2_l, geo, n_tiles)
        xb = None
    return xa.reshape(bsz, s_len, d)
```
